```python
import jax, jax.numpy as jnp
from jax import lax
import numpy as np

D_MODEL = 1024
BATCH = 8
SEQ = 2048
DEPTH = 4

CHUNK = 64
GM_BLOCK = 128
GM_HEADS = 8
GM_WIDTH = 512
GM_HEAD_DIM = GM_WIDTH // GM_HEADS
CF_WIDTH = 512
CF_KERNEL = 31
SC_WIDTH = 512
SC_KERNEL = 3
N_BRANCH = 3
N_GROUPS = 4
EXPERTS_PER_GROUP = 4
N_EXPERTS = N_GROUPS * EXPERTS_PER_GROUP
TOP_K_IN_GROUP = 2
D_EXPERT = 512
EPS = 1e-6
SPLITS = tuple(int(s) for s in np.cumsum([GM_WIDTH, GM_WIDTH, CF_WIDTH, CF_WIDTH, SC_WIDTH, SC_WIDTH, SC_WIDTH]))
W_IN_COLS = 2 * GM_WIDTH + 2 * CF_WIDTH + 3 * SC_WIDTH + N_BRANCH * D_MODEL

kernel_name = 'hybrid_gated_gmlp_conformer_shortconv_hiermoe'


def rmsnorm(x, g):
    xf = x.astype(jnp.float32)
    y = xf * lax.rsqrt(jnp.mean(xf * xf, axis=-1, keepdims=True) + EPS)
    return (y * g.astype(jnp.float32)).astype(x.dtype)


def layernorm(x, g, b):
    xf = x.astype(jnp.float32)
    mu = jnp.mean(xf, axis=-1, keepdims=True)
    var = jnp.mean(jnp.square(xf - mu), axis=-1, keepdims=True)
    y = (xf - mu) * lax.rsqrt(var + EPS)
    return (y * g.astype(jnp.float32) + b.astype(jnp.float32)).astype(x.dtype)


def causal_dwconv(x, w):
    k, c = w.shape
    return lax.conv_general_dilated(
        x, w[:, None, :].astype(x.dtype), window_strides=(1,), padding=[(k - 1, 0)],
        dimension_numbers=('NWC', 'WIO', 'NWC'), feature_group_count=c)


def hybrid_mixer(h, w_in, gm_vnorm_g, gm_ws, gm_bs, gm_w_out, cf_dw_w, cf_dw_b,
                 cf_ln_g, cf_ln_b, cf_w_out, sc_conv_w, sc_w_out, w_o):
    b, s, _ = h.shape
    proj = h @ w_in
    gm_u, gm_v, cf_a, cf_gate, sc_b, sc_c, sc_h, gates = jnp.split(proj, SPLITS, axis=-1)

    v = rmsnorm(gm_v.reshape(b, s, GM_HEADS, GM_HEAD_DIM), gm_vnorm_g.reshape(GM_HEADS, GM_HEAD_DIM))
    v = v.reshape(b, s // GM_BLOCK, GM_BLOCK, GM_HEADS, GM_HEAD_DIM)
    chunk_id = jnp.arange(GM_BLOCK) // CHUNK
    mask = chunk_id[None, :] <= chunk_id[:, None]
    ws = jnp.where(mask[None], gm_ws, jnp.zeros_like(gm_ws))
    z = jnp.einsum('hij,bnjhc->bnihc', ws, v) + gm_bs.T[None, None, :, :, None]
    y_gm = (gm_u * z.reshape(b, s, GM_WIDTH)) @ gm_w_out

    a = cf_a * jax.nn.sigmoid(cf_gate)
    a = causal_dwconv(a, cf_dw_w) + cf_dw_b
    a = jax.nn.silu(layernorm(a, cf_ln_g, cf_ln_b))
    y_cf = a @ cf_w_out

    y_sc = (sc_b * causal_dwconv(sc_c * sc_h, sc_conv_w)) @ sc_w_out

    g = jax.nn.sigmoid(gates.reshape(b, s, N_BRANCH, D_MODEL))
    m = g[..., 0, :] * y_gm + g[..., 1, :] * y_cf + g[..., 2, :] * y_sc
    return m @ w_o


def hier_moe(h, router_g, router_g_b, router_e, router_e_b, w_gate, w_up, w_down):
    b, s, d = h.shape
    xf = h.reshape(b * s, d)
    lg = (xf @ router_g + router_g_b).astype(jnp.float32)
    pg = jax.nn.softmax(lg, axis=-1)
    g_idx = jnp.argmax(lg, axis=-1)
    p_top = jnp.take_along_axis(pg, g_idx[:, None], axis=-1)
    le = (xf @ router_e + router_e_b).astype(jnp.float32).reshape(-1, N_GROUPS, EXPERTS_PER_GROUP)
    le_sel = jnp.take_along_axis(le, g_idx[:, None, None], axis=1)[:, 0]
    top_v, top_i = lax.top_k(le_sel, TOP_K_IN_GROUP)
    w_k = jax.nn.softmax(top_v, axis=-1) * p_top
    eid = g_idx[:, None] * EXPERTS_PER_GROUP + top_i
    gates = jnp.einsum('nk,nke->ne', w_k, jax.nn.one_hot(eid, N_EXPERTS, dtype=jnp.float32)).astype(h.dtype)
    y = jnp.zeros_like(xf)
    for e in range(N_EXPERTS):
        act = jax.nn.silu(xf @ w_gate[e]) * (xf @ w_up[e])
        y = y + (gates[:, e:e + 1] * act) @ w_down[e]
    return y.reshape(b, s, d)


def setup_inputs(seed: int = 0) -> dict:
    key = jax.random.key(seed)
    ks = jax.random.split(key, 24)
    L, D = DEPTH, D_MODEL
    nrm = lambda k, shape, scale: jax.random.normal(k, shape, jnp.float32) * scale
    gain = lambda k, shape: 1.0 + 0.05 * jax.random.normal(k, shape, jnp.float32)
    return {
        'x': jax.random.normal(ks[0], (BATCH, SEQ, D), jnp.float32),
        'norm1_g': gain(ks[1], (L, D)),
        'w_in': nrm(ks[2], (L, D, W_IN_COLS), D ** -0.5),
        'gm_vnorm_g': gain(ks[3], (L, GM_WIDTH)),
        'gm_ws': nrm(ks[4], (L, GM_HEADS, GM_BLOCK, GM_BLOCK), GM_BLOCK ** -0.5),
        'gm_bs': 1.0 + 0.1 * jax.random.normal(ks[5], (L, GM_HEADS, GM_BLOCK), jnp.float32),
        'gm_w_out': nrm(ks[6], (L, GM_WIDTH, D), GM_WIDTH ** -0.5),
        'cf_dw_w': nrm(ks[7], (L, CF_KERNEL, CF_WIDTH), CF_KERNEL ** -0.5),
        'cf_dw_b': nrm(ks[8], (L, CF_WIDTH), 0.02),
        'cf_ln_g': gain(ks[9], (L, CF_WIDTH)),
        'cf_ln_b': nrm(ks[10], (L, CF_WIDTH), 0.02),
        'cf_w_out': nrm(ks[11], (L, CF_WIDTH, D), CF_WIDTH ** -0.5),
        'sc_conv_w': nrm(ks[12], (L, SC_KERNEL, SC_WIDTH), SC_KERNEL ** -0.5),
        'sc_w_out': nrm(ks[13], (L, SC_WIDTH, D), SC_WIDTH ** -0.5),
        'w_o': nrm(ks[14], (L, D, D), D ** -0.5),
        'norm2_g': gain(ks[15], (L, D)),
        'router_g': nrm(ks[16], (L, D, N_GROUPS), D ** -0.5),
        'router_g_b': nrm(ks[17], (L, N_GROUPS), 0.01),
        'router_e': nrm(ks[18], (L, D, N_EXPERTS), D ** -0.5),
        'router_e_b': nrm(ks[19], (L, N_EXPERTS), 0.01),
        'exp_w_gate': nrm(ks[20], (L, N_EXPERTS, D, D_EXPERT), D ** -0.5),
        'exp_w_up': nrm(ks[21], (L, N_EXPERTS, D, D_EXPERT), D ** -0.5),
        'exp_w_down': nrm(ks[22], (L, N_EXPERTS, D_EXPERT, D), D_EXPERT ** -0.5),
        'final_norm_g': gain(ks[23], (D,)),
    }


def reference(x, norm1_g, w_in, gm_vnorm_g, gm_ws, gm_bs, gm_w_out, cf_dw_w, cf_dw_b,
              cf_ln_g, cf_ln_b, cf_w_out, sc_conv_w, sc_w_out, w_o, norm2_g,
              router_g, router_g_b, router_e, router_e_b, exp_w_gate, exp_w_up,
              exp_w_down, final_norm_g):
    for i in range(DEPTH):
        h = rmsnorm(x, norm1_g[i])
        x = x + hybrid_mixer(h, w_in[i], gm_vnorm_g[i], gm_ws[i], gm_bs[i], gm_w_out[i],
                             cf_dw_w[i], cf_dw_b[i], cf_ln_g[i], cf_ln_b[i], cf_w_out[i],
                             sc_conv_w[i], sc_w_out[i], w_o[i])
        h = rmsnorm(x, norm2_g[i])
        x = x + hier_moe(h, router_g[i], router_g_b[i], router_e[i], router_e_b[i],
                         exp_w_gate[i], exp_w_up[i], exp_w_down[i])
    return rmsnorm(x, final_norm_g)
```

```python
import functools

import numpy as np
import jax
import jax.numpy as jnp
from jax import lax
from jax.experimental import pallas as pl
from jax.experimental.pallas import tpu as pltpu

F32 = jnp.float32
BF16 = jnp.bfloat16

D_MODEL = 1024
CHUNK = 64
GM_BLOCK = 128
GM_HEADS = 8
GM_WIDTH = 512
GM_HEAD_DIM = GM_WIDTH // GM_HEADS
CF_WIDTH = 512
CF_KERNEL = 31
SC_WIDTH = 512
SC_KERNEL = 3
N_BRANCH = 3
N_GROUPS = 4
EXPERTS_PER_GROUP = 4
N_EXPERTS = N_GROUPS * EXPERTS_PER_GROUP
D_EXPERT = 512
EPS = 1e-6

C_GU, C_GV, C_CA, C_CG, C_SB, C_SC, C_SH, C_GATES = 0, 512, 1024, 1536, 2048, 2560, 3072, 3584
W_IN_COLS = C_GATES + N_BRANCH * D_MODEL

LANES = 128
SUBLANES = 8
N_PAIRS = 6
N_BUCKETS = N_GROUPS * N_PAIRS
PAIR_LO = (0, 0, 0, 1, 1, 2)
PAIR_HI = (1, 2, 3, 2, 3, 3)

TM = 256
TR = 256
CF_TAIL = 32
SC_TAIL = 8
CONV_ROWS = 32
VMEM_LIMIT = 56 * 1024 * 1024


def _dot(a, b):
    return jnp.dot(a, b, preferred_element_type=F32)


def _split_bf16(v):
    hi = v.astype(BF16)
    lo = (v - hi.astype(F32)).astype(BF16)
    return hi, lo


def _router_logits(x2, n2g, wr_hi, wr_lo, rb):
    h2 = x2 * lax.rsqrt(jnp.mean(x2 * x2, axis=-1, keepdims=True) + EPS) * n2g
    hi, lo = _split_bf16(h2)
    logits = _dot(hi, wr_hi) + _dot(lo, wr_hi) + _dot(hi, wr_lo) + rb
    return h2, logits


def _mixer_kernel(x_ref, n1g_ref, w_in_ref, vng_ref, gmat_ref, ws_ref, gbias_ref, gm_wo_ref,
                  dww_ref, dwb_ref, lng_ref, lnb_ref, cf_wo_ref, scw_ref, sc_wo_ref, w_o_ref,
                  n2g_ref, wr_hi_ref, wr_lo_ref, rb_ref,
                  x2_ref, route_ref, cnt_ref,
                  abuf, shbuf, cvbuf, sbuf, base, *, tiles_per_seq):
    i = pl.program_id(0)
    first = (i % tiles_per_seq) == 0
    nb = TM // GM_BLOCK

    x = x_ref[...]
    h = x * lax.rsqrt(jnp.mean(x * x, axis=-1, keepdims=True) + EPS) * n1g_ref[...]
    hb = h.astype(BF16)

    pu = _dot(hb, w_in_ref[:, C_GU:C_GU + GM_WIDTH])
    pv = _dot(hb, w_in_ref[:, C_GV:C_GV + GM_WIDTH])
    sq_hi, sq_lo = _split_bf16(pv * pv)
    ms = _dot(sq_hi, gmat_ref[...]) + _dot(sq_lo, gmat_ref[...])
    v = (pv * lax.rsqrt(ms + EPS) * vng_ref[...]).astype(BF16)

    ri = lax.broadcasted_iota(jnp.int32, (GM_BLOCK, GM_BLOCK), 0) // CHUNK
    ci = lax.broadcasted_iota(jnp.int32, (GM_BLOCK, GM_BLOCK), 1) // CHUNK
    causal = ci <= ri
    lane_in_pair = lax.broadcasted_iota(jnp.int32, (GM_BLOCK, nb * LANES), 1) % LANES
    first_head = lane_in_pair < GM_HEAD_DIM
    zparts = []
    for p in range(GM_HEADS // 2):
        w2 = jnp.concatenate(
            [jnp.where(causal, ws_ref[2 * p], 0.0), jnp.where(causal, ws_ref[2 * p + 1], 0.0)],
            axis=0).astype(BF16)
        vcat = jnp.concatenate(
            [v[n * GM_BLOCK:(n + 1) * GM_BLOCK, p * LANES:(p + 1) * LANES] for n in range(nb)], axis=1)
        r = _dot(w2, vcat)
        zparts.append(jnp.where(first_head, r[:GM_BLOCK], r[GM_BLOCK:]))
    z = jnp.concatenate(
        [jnp.concatenate([zp[:, n * LANES:(n + 1) * LANES] for zp in zparts], axis=1) + gbias_ref[...]
         for n in range(nb)], axis=0)
    y_gm = _dot((pu * z).astype(BF16), gm_wo_ref[...])

    ca = _dot(hb, w_in_ref[:, C_CA:C_CA + CF_WIDTH])
    cg = _dot(hb, w_in_ref[:, C_CG:C_CG + CF_WIDTH])

    @pl.when(first)
    def _():
        abuf[0:CF_TAIL, :] = jnp.zeros((CF_TAIL, CF_WIDTH), F32)
        sbuf[0:SC_TAIL, :] = jnp.zeros((SC_TAIL, SC_WIDTH), F32)

    @pl.when(jnp.logical_not(first))
    def _():
        abuf[0:CF_TAIL, :] = abuf[TM:TM + CF_TAIL, :]
        sbuf[0:SC_TAIL, :] = sbuf[TM:TM + SC_TAIL, :]

    abuf[CF_TAIL:CF_TAIL + TM, :] = ca * jax.nn.sigmoid(cg)
    for r in range(1, SUBLANES):
        shbuf[r - 1, 0:TM + CF_TAIL - SUBLANES, :] = abuf[r:r + TM + CF_TAIL - SUBLANES, :]

    def conv_chunk(c, carry):
        row0 = pl.multiple_of(c * CONV_ROWS, CONV_ROWS)
        ngrp = CONV_ROWS // SUBLANES
        acc = [jnp.zeros((SUBLANES, CF_WIDTH), F32) for _ in range(ngrp)]
        for k in range(CF_KERNEL):
            s = k + CF_TAIL - (CF_KERNEL - 1)
            q, r = divmod(s, SUBLANES)
            wk = dww_ref[k * SUBLANES:(k + 1) * SUBLANES, :]
            for g in range(ngrp):
                start = row0 + (q + g) * SUBLANES
                if r == 0:
                    val = abuf[pl.ds(start, SUBLANES), :]
                else:
                    val = shbuf[r - 1, pl.ds(start, SUBLANES), :]
                acc[g] = acc[g] + wk * val
        for g in range(ngrp):
            cvbuf[pl.ds(row0 + g * SUBLANES, SUBLANES), :] = acc[g]
        return carry

    lax.fori_loop(0, TM // CONV_ROWS, conv_chunk, 0)
    cv = cvbuf[...] + dwb_ref[...]
    mu = jnp.mean(cv, axis=-1, keepdims=True)
    cc = cv - mu
    var = jnp.mean(cc * cc, axis=-1, keepdims=True)
    ln = cc * lax.rsqrt(var + EPS) * lng_ref[...] + lnb_ref[...]
    y_cf = _dot(jax.nn.silu(ln).astype(BF16), cf_wo_ref[...])

    s_b = _dot(hb, w_in_ref[:, C_SB:C_SB + SC_WIDTH])
    s_c = _dot(hb, w_in_ref[:, C_SC:C_SC + SC_WIDTH])
    s_h = _dot(hb, w_in_ref[:, C_SH:C_SH + SC_WIDTH])
    sbuf[SC_TAIL:SC_TAIL + TM, :] = s_c * s_h
    conv3 = (scw_ref[0:1, :] * sbuf[SC_TAIL - 2:SC_TAIL - 2 + TM, :]
             + scw_ref[1:2, :] * sbuf[SC_TAIL - 1:SC_TAIL - 1 + TM, :]
             + scw_ref[2:3, :] * sbuf[SC_TAIL:SC_TAIL + TM, :])
    y_sc = _dot((s_b * conv3).astype(BF16), sc_wo_ref[...])

    m = jax.nn.sigmoid(_dot(hb, w_in_ref[:, C_GATES:C_GATES + D_MODEL])) * y_gm
    m = m + jax.nn.sigmoid(_dot(hb, w_in_ref[:, C_GATES + D_MODEL:C_GATES + 2 * D_MODEL])) * y_cf
    m = m + jax.nn.sigmoid(_dot(hb, w_in_ref[:, C_GATES + 2 * D_MODEL:C_GATES + 3 * D_MODEL])) * y_sc
    x2 = x + _dot(m.astype(BF16), w_o_ref[...])
    x2_ref[...] = x2

    _, logits = _router_logits(x2, n2g_ref[...], wr_hi_ref[...], wr_lo_ref[...], rb_ref[...])
    lane = lax.broadcasted_iota(jnp.int32, (TM, LANES), 1).astype(F32)
    neg = jnp.float32(-jnp.inf)
    far = jnp.float32(1e9)
    is_g = lane < N_GROUPS
    gmax = jnp.max(jnp.where(is_g, logits, neg), axis=-1, keepdims=True)
    gidx = jnp.min(jnp.where(is_g & (logits == gmax), lane, far), axis=-1, keepdims=True)
    e_lo = N_GROUPS + EXPERTS_PER_GROUP * gidx
    in_grp = (lane >= e_lo) & (lane < e_lo + EXPERTS_PER_GROUP)
    el = jnp.where(in_grp, logits, neg)
    v1 = jnp.max(el, axis=-1, keepdims=True)
    i1 = jnp.min(jnp.where(in_grp & (logits == v1), lane, far), axis=-1, keepdims=True)
    rest = in_grp & (lane != i1)
    v2 = jnp.max(jnp.where(rest, logits, neg), axis=-1, keepdims=True)
    i2 = jnp.min(jnp.where(rest & (logits == v2), lane, far), axis=-1, keepdims=True)
    p_lo = jnp.minimum(i1, i2) - e_lo
    p_hi = jnp.maximum(i1, i2) - e_lo
    pair = p_lo * (7.0 - p_lo) * 0.5 + p_hi - p_lo - 1.0
    bucket = gidx * N_PAIRS + pair

    @pl.when(i == 0)
    def _():
        base[...] = jnp.zeros((1, LANES), F32)

    hit = lane == bucket
    onehot = hit.astype(BF16)
    tri = (lax.broadcasted_iota(jnp.int32, (TM, TM), 0)
           >= lax.broadcasted_iota(jnp.int32, (TM, TM), 1)).astype(BF16)
    prefix = _dot(tri, onehot)
    rank = jnp.sum(jnp.where(hit, prefix - 1.0 + base[...], 0.0), axis=-1, keepdims=True)
    base[...] = base[...] + prefix[TM - 1:TM, :]
    route = jnp.where(lane == 0.0, bucket, jnp.where(lane == 1.0, rank, 0.0))
    route_ref[...] = route.astype(jnp.int32)
    cnt_ref[...] = base[...].astype(jnp.int32)


def _mixer_call(x, layer, p, tiles_per_seq):
    n = p["n_tokens"]
    grid = (n // TM,)

    def const2(shape):
        return pl.BlockSpec(shape, lambda i: (0, 0))

    def layer2(shape):
        return pl.BlockSpec((None,) + shape, lambda i: (layer, 0, 0))

    def layer3(shape):
        return pl.BlockSpec((None,) + shape, lambda i: (layer, 0, 0, 0))

    in_specs = [
        pl.BlockSpec((TM, D_MODEL), lambda i: (i, 0)),
        layer2((1, D_MODEL)),
        layer2((D_MODEL, W_IN_COLS)),
        layer2((1, GM_WIDTH)),
        const2((GM_WIDTH, GM_WIDTH)),
        layer3((GM_HEADS, GM_BLOCK, GM_BLOCK)),
        layer2((GM_BLOCK, GM_WIDTH)),
        layer2((GM_WIDTH, D_MODEL)),
        layer2((CF_KERNEL * SUBLANES, CF_WIDTH)),
        layer2((1, CF_WIDTH)),
        layer2((1, CF_WIDTH)),
        layer2((1, CF_WIDTH)),
        layer2((CF_WIDTH, D_MODEL)),
        layer2((SC_KERNEL, SC_WIDTH)),
        layer2((SC_WIDTH, D_MODEL)),
        layer2((D_MODEL, D_MODEL)),
        layer2((1, D_MODEL)),
        layer2((D_MODEL, LANES)),
        layer2((D_MODEL, LANES)),
        layer2((1, LANES)),
    ]
    out_specs = [
        pl.BlockSpec((TM, D_MODEL), lambda i: (i, 0)),
        pl.BlockSpec((TM, LANES), lambda i: (i, 0)),
        pl.BlockSpec((1, LANES), lambda i: (0, 0)),
    ]
    out_shape = [
        jax.ShapeDtypeStruct((n, D_MODEL), F32),
        jax.ShapeDtypeStruct((n, LANES), jnp.int32),
        jax.ShapeDtypeStruct((1, LANES), jnp.int32),
    ]
    scratch = [
        pltpu.VMEM((TM + CF_TAIL, CF_WIDTH), F32),
        pltpu.VMEM((SUBLANES - 1, TM + CF_TAIL, CF_WIDTH), F32),
        pltpu.VMEM((TM, CF_WIDTH), F32),
        pltpu.VMEM((TM + SC_TAIL, SC_WIDTH), F32),
        pltpu.VMEM((1, LANES), F32),
    ]
    return pl.pallas_call(
        functools.partial(_mixer_kernel, tiles_per_seq=tiles_per_seq),
        grid=grid, in_specs=in_specs, out_specs=out_specs, out_shape=out_shape,
        scratch_shapes=scratch,
        compiler_params=pltpu.CompilerParams(
            dimension_semantics=("arbitrary",), vmem_limit_bytes=VMEM_LIMIT),
        name=f"mixer_l{layer}",
    )(x, p["norm1_g"], p["w_in"], p["gm_vnorm_g"], p["gmat"], p["gm_ws"], p["gm_bias"], p["gm_w_out"],
      p["cf_dw_w"], p["cf_dw_b"], p["cf_ln_g"], p["cf_ln_b"], p["cf_w_out"], p["sc_conv_w"],
      p["sc_w_out"], p["w_o"], p["norm2_g"], p["wr_hi"], p["wr_lo"], p["rb"])


def _moe_kernel(tinfo_ref, nt_ref, src_ref,
                x2_hbm, n2g_ref, wr_hi_ref, wr_lo_ref, rb_ref,
                wga_ref, wua_ref, wda_ref, wgb_ref, wub_ref, wdb_ref, fg_ref,
                out_hbm,
                xbuf, obuf, gsem, ssem, *, final):
    t = pl.program_id(0)
    nt = nt_ref[0]
    slot = t % 2

    def gather_row(tile, sl, r):
        tok = src_ref[tile * TR + r]
        return pltpu.make_async_copy(x2_hbm.at[pl.ds(tok, 1)], xbuf.at[sl, pl.ds(r, 1)], gsem.at[sl])

    def scatter_row(tile, sl, r):
        tok = src_ref[tile * TR + r]
        return pltpu.make_async_copy(obuf.at[sl, pl.ds(r, 1)], out_hbm.at[pl.ds(tok, 1)], ssem.at[sl])

    def start_rows(make, tile, sl, count):
        def body(r, c):
            make(tile, sl, r).start()
            return c
        lax.fori_loop(0, count, body, 0)

    def wait_rows(make, tile, sl, count):
        def body(r, c):
            make(tile, sl, r).wait()
            return c
        lax.fori_loop(0, count, body, 0)

    def valid_rows(tile):
        return tinfo_ref[4 * tile + 3]

    @pl.when(t < nt)
    def _():
        @pl.when(t == 0)
        def _():
            start_rows(gather_row, t, slot, TR)

        @pl.when(t + 1 < nt)
        def _():
            start_rows(gather_row, t + 1, 1 - slot, TR)

        wait_rows(gather_row, t, slot, TR)

        @pl.when(t >= 2)
        def _():
            wait_rows(scatter_row, t - 2, slot, valid_rows(t - 2))

        grp = tinfo_ref[4 * t]
        ea = tinfo_ref[4 * t + 1]
        eb = tinfo_ref[4 * t + 2]
        xr = xbuf[slot]
        h2, logits = _router_logits(xr, n2g_ref[...], wr_hi_ref[...], wr_lo_ref[...], rb_ref[...])
        lane = lax.broadcasted_iota(jnp.int32, (TR, LANES), 1)
        is_g = lane < N_GROUPS
        gmax = jnp.max(jnp.where(is_g, logits, -jnp.inf), axis=-1, keepdims=True)
        eg = jnp.where(is_g, jnp.exp(logits - gmax), 0.0)
        p_top = (jnp.sum(jnp.where(lane == grp, eg, 0.0), axis=-1, keepdims=True)
                 / jnp.sum(eg, axis=-1, keepdims=True))
        la = jnp.sum(jnp.where(lane == N_GROUPS + ea, logits, 0.0), axis=-1, keepdims=True)
        lb = jnp.sum(jnp.where(lane == N_GROUPS + eb, logits, 0.0), axis=-1, keepdims=True)
        mx = jnp.maximum(la, lb)
        pa = jnp.exp(la - mx)
        pb = jnp.exp(lb - mx)
        wa = pa / (pa + pb) * p_top
        wb = pb / (pa + pb) * p_top

        hb = h2.astype(BF16)
        act_a = jax.nn.silu(_dot(hb, wga_ref[...])) * _dot(hb, wua_ref[...])
        act_b = jax.nn.silu(_dot(hb, wgb_ref[...])) * _dot(hb, wub_ref[...])
        y = _dot((wa * act_a).astype(BF16), wda_ref[...]) + _dot((wb * act_b).astype(BF16), wdb_ref[...])
        res = xr + y
        if final:
            res = res * lax.rsqrt(jnp.mean(res * res, axis=-1, keepdims=True) + EPS) * fg_ref[...]
        obuf[slot] = res

        start_rows(scatter_row, t, slot, valid_rows(t))

        @pl.when(t == nt - 1)
        def _():
            @pl.when(t >= 1)
            def _():
                wait_rows(scatter_row, t - 1, 1 - slot, valid_rows(t - 1))
            wait_rows(scatter_row, t, slot, valid_rows(t))


def _moe_call(x2, layer, p, tinfo, nt, src, final):
    n = p["n_tokens"]
    max_tiles = n // TR + N_BUCKETS

    def const2(shape):
        return pl.BlockSpec(shape, lambda t, *_: (0, 0))

    def layer2(shape):
        return pl.BlockSpec((None,) + shape, lambda t, *_: (layer, 0, 0))

    def expert(shape, which):
        def imap(t, tinfo_ref, nt_ref, src_ref):
            tt = jnp.minimum(t, nt_ref[0] - 1)
            return (layer, tinfo_ref[4 * tt + which], 0, 0)
        return pl.BlockSpec((None, None) + shape, imap)

    in_specs = [
        pl.BlockSpec(memory_space=pl.ANY),
        layer2((1, D_MODEL)),
        layer2((D_MODEL, LANES)),
        layer2((D_MODEL, LANES)),
        layer2((1, LANES)),
        expert((D_MODEL, D_EXPERT), 1),
        expert((D_MODEL, D_EXPERT), 1),
        expert((D_EXPERT, D_MODEL), 1),
        expert((D_MODEL, D_EXPERT), 2),
        expert((D_MODEL, D_EXPERT), 2),
        expert((D_EXPERT, D_MODEL), 2),
        const2((1, D_MODEL)),
    ]
    grid_spec = pltpu.PrefetchScalarGridSpec(
        num_scalar_prefetch=3, grid=(max_tiles,), in_specs=in_specs,
        out_specs=pl.BlockSpec(memory_space=pl.ANY),
        scratch_shapes=[
            pltpu.VMEM((2, TR, D_MODEL), F32),
            pltpu.VMEM((2, TR, D_MODEL), F32),
            pltpu.SemaphoreType.DMA((2,)),
            pltpu.SemaphoreType.DMA((2,)),
        ])
    return pl.pallas_call(
        functools.partial(_moe_kernel, final=final),
        grid_spec=grid_spec,
        out_shape=jax.ShapeDtypeStruct((n, D_MODEL), F32),
        compiler_params=pltpu.CompilerParams(
            dimension_semantics=("arbitrary",), vmem_limit_bytes=VMEM_LIMIT),
        name=f"moe_l{layer}",
    )(tinfo, nt, src, x2, p["norm2_g"], p["wr_hi"], p["wr_lo"], p["rb"],
      p["exp_w_gate"], p["exp_w_up"], p["exp_w_down"], p["exp_w_gate"], p["exp_w_up"], p["exp_w_down"],
      p["final_norm_g"])


def _dispatch_tables(route, counts, n):
    max_tiles = n // TR + N_BUCKETS
    bucket = route[:, 0]
    rank = route[:, 1]
    cnt = counts[0, :N_BUCKETS]
    tiles = (cnt + TR - 1) // TR
    tile_end = jnp.cumsum(tiles)
    tile_start = tile_end - tiles
    nt = tile_end[-1:].astype(jnp.int32)
    pos = tile_start[bucket] * TR + rank
    slots = max_tiles * TR
    src = jnp.zeros((slots,), jnp.int32).at[pos].set(jnp.arange(n, dtype=jnp.int32))
    tile_ids = jnp.arange(max_tiles, dtype=jnp.int32)
    tb = jnp.sum((tile_ids[:, None] >= tile_end[None, :]).astype(jnp.int32), axis=1)
    tb = jnp.minimum(tb, N_BUCKETS - 1)
    grp = tb // N_PAIRS
    pr = tb % N_PAIRS
    ea = grp * EXPERTS_PER_GROUP + jnp.asarray(PAIR_LO, jnp.int32)[pr]
    eb = grp * EXPERTS_PER_GROUP + jnp.asarray(PAIR_HI, jnp.int32)[pr]
    valid = jnp.clip(cnt[tb] - (tile_ids - tile_start[tb]) * TR, 0, TR)
    tinfo = jnp.stack([grp, ea, eb, valid], axis=1).reshape(-1).astype(jnp.int32)
    return tinfo, nt, src


def kernel(x, norm1_g, w_in, gm_vnorm_g, gm_ws, gm_bs, gm_w_out, cf_dw_w, cf_dw_b, cf_ln_g, cf_ln_b,
           cf_w_out, sc_conv_w, sc_w_out, w_o, norm2_g, router_g, router_g_b, router_e, router_e_b,
           exp_w_gate, exp_w_up, exp_w_down, final_norm_g):
    b, s, d = x.shape
    depth = w_in.shape[0]
    n = b * s
    assert d == D_MODEL and s % TM == 0 and TM % GM_BLOCK == 0 and n % TR == 0

    wr = jnp.concatenate([router_g, router_e], axis=-1)
    wr = jnp.pad(wr, ((0, 0), (0, 0), (0, LANES - wr.shape[-1])))
    wr_hi = wr.astype(BF16)
    wr_lo = (wr - wr_hi.astype(F32)).astype(BF16)
    rb = jnp.concatenate([router_g_b, router_e_b], axis=-1)
    rb = jnp.pad(rb, ((0, 0), (0, LANES - rb.shape[-1])))[:, None, :]
    head = np.arange(GM_WIDTH) // GM_HEAD_DIM
    gmat = jnp.asarray((head[:, None] == head[None, :]).astype(np.float32) / GM_HEAD_DIM, BF16)

    p = dict(
        n_tokens=n,
        norm1_g=norm1_g[:, None, :],
        w_in=w_in.astype(BF16),
        gm_vnorm_g=gm_vnorm_g[:, None, :],
        gmat=gmat,
        gm_ws=gm_ws,
        gm_bias=jnp.repeat(jnp.swapaxes(gm_bs, 1, 2), GM_HEAD_DIM, axis=2),
        gm_w_out=gm_w_out.astype(BF16),
        cf_dw_w=jnp.repeat(cf_dw_w, SUBLANES, axis=1),
        cf_dw_b=cf_dw_b[:, None, :],
        cf_ln_g=cf_ln_g[:, None, :],
        cf_ln_b=cf_ln_b[:, None, :],
        cf_w_out=cf_w_out.astype(BF16),
        sc_conv_w=sc_conv_w,
        sc_w_out=sc_w_out.astype(BF16),
        w_o=w_o.astype(BF16),
        norm2_g=norm2_g[:, None, :],
        wr_hi=wr_hi, wr_lo=wr_lo, rb=rb,
        exp_w_gate=exp_w_gate.astype(BF16),
        exp_w_up=exp_w_up.astype(BF16),
        exp_w_down=exp_w_down.astype(BF16),
        final_norm_g=final_norm_g[None, :],
    )

    xf = x.reshape(n, d)
    for layer in range(depth):
        x2, route, counts = _mixer_call(xf, layer, p, s // TM)
        tinfo, nt, src = _dispatch_tables(route, counts, n)
        xf = _moe_call(x2, layer, p, tinfo, nt, src, final=(layer == depth - 1))
    return xf.reshape(b, s, d)
```

```python
import functools

import numpy as np
import jax
import jax.numpy as jnp
from jax import lax
from jax.experimental import pallas as pl
from jax.experimental.pallas import tpu as pltpu

F32 = jnp.float32
BF16 = jnp.bfloat16

D_MODEL = 1024
CHUNK = 64
GM_BLOCK = 128
GM_HEADS = 8
GM_WIDTH = 512
GM_HEAD_DIM = GM_WIDTH // GM_HEADS
CF_WIDTH = 512
CF_KERNEL = 31
SC_WIDTH = 512
SC_KERNEL = 3
N_BRANCH = 3
N_GROUPS = 4
EXPERTS_PER_GROUP = 4
N_EXPERTS = N_GROUPS * EXPERTS_PER_GROUP
D_EXPERT = 512
EPS = 1e-6

C_GU, C_GV, C_CA, C_CG, C_SB, C_SC, C_SH, C_GATES = 0, 512, 1024, 1536, 2048, 2560, 3072, 3584
W_IN_COLS = C_GATES + N_BRANCH * D_MODEL

LANES = 128
SUBLANES = 8
TOKEN_ROWS = D_MODEL // LANES
N_PAIRS = 6
N_BUCKETS = N_GROUPS * N_PAIRS
PAIR_A = (0, 0, 0, 1, 1, 3)
PAIR_B = (1, 2, 3, 2, 3, 2)
PAIR_ORDER = (0, 1, 2, 4, 3, 5)
RANK_BITS = 16

TM = 256
TR = 256
CF_TAIL = 32
SC_TAIL = 8
CONV_ROWS = 32
DMA_UNROLL = 8
VMEM_LIMIT = 56 * 1024 * 1024


def _dot(a, b):
    return jnp.dot(a, b, preferred_element_type=F32)


def _split_bf16(v):
    hi = v.astype(BF16)
    lo = (v - hi.astype(F32)).astype(BF16)
    return hi, lo


def _router_logits(x2, n2g, wr_hi, wr_lo, rb):
    h2 = x2 * lax.rsqrt(jnp.mean(x2 * x2, axis=-1, keepdims=True) + EPS) * n2g
    hi, lo = _split_bf16(h2)
    logits = _dot(hi, wr_hi) + _dot(lo, wr_hi) + _dot(hi, wr_lo) + rb
    return h2, logits


def _load_token_rows(ref, row0, n):
    return jnp.concatenate(
        [ref[pl.ds(row0 + s, n, stride=TOKEN_ROWS), :] for s in range(TOKEN_ROWS)], axis=1)


def _store_token_rows(ref, row0, val):
    n = val.shape[0]
    for s in range(TOKEN_ROWS):
        ref[pl.ds(row0 + s, n, stride=TOKEN_ROWS), :] = val[:, s * LANES:(s + 1) * LANES]


def _mixer_kernel(x_ref, n1g_ref, w_in_ref, vng_ref, gmat_ref, ws_ref, gbias_ref, gm_wo_ref,
                  dww_ref, dwb_ref, lng_ref, lnb_ref, cf_wo_ref, scw_ref, sc_wo_ref, w_o_ref,
                  n2g_ref, wr_hi_ref, wr_lo_ref, rb_ref,
                  x2_ref, route_ref, cnt_ref,
                  abuf, shbuf, sbuf, base, *, tiles_per_seq, rows_in):
    i = pl.program_id(0)
    first = (i % tiles_per_seq) == 0
    nb = TM // GM_BLOCK

    @pl.when(first)
    def _():
        abuf[0:CF_TAIL, :] = jnp.zeros((CF_TAIL, CF_WIDTH), F32)
        sbuf[0:SC_TAIL, :] = jnp.zeros((SC_TAIL, SC_WIDTH), F32)

    @pl.when(jnp.logical_not(first))
    def _():
        abuf[0:CF_TAIL, :] = abuf[TM:TM + CF_TAIL, :]
        sbuf[0:SC_TAIL, :] = sbuf[TM:TM + SC_TAIL, :]

    @pl.when(i == 0)
    def _():
        base[...] = jnp.zeros((1, LANES), F32)

    x = _load_token_rows(x_ref, 0, TM) if rows_in else x_ref[...]
    h = x * lax.rsqrt(jnp.mean(x * x, axis=-1, keepdims=True) + EPS) * n1g_ref[...]
    hb = h.astype(BF16)

    pu = _dot(hb, w_in_ref[:, C_GU:C_GU + GM_WIDTH])
    pv = _dot(hb, w_in_ref[:, C_GV:C_GV + GM_WIDTH])
    sq_hi, sq_lo = _split_bf16(pv * pv)
    ms = _dot(sq_hi, gmat_ref[...]) + _dot(sq_lo, gmat_ref[...])
    v = (pv * lax.rsqrt(ms + EPS) * vng_ref[...]).astype(BF16)

    ri = lax.broadcasted_iota(jnp.int32, (GM_BLOCK, GM_BLOCK), 0) // CHUNK
    ci = lax.broadcasted_iota(jnp.int32, (GM_BLOCK, GM_BLOCK), 1) // CHUNK
    causal = ci <= ri
    lane_in_pair = lax.broadcasted_iota(jnp.int32, (GM_BLOCK, nb * LANES), 1) % LANES
    first_head = lane_in_pair < GM_HEAD_DIM
    zparts = []
    for p in range(GM_HEADS // 2):
        w2 = jnp.concatenate(
            [jnp.where(causal, ws_ref[2 * p], 0.0), jnp.where(causal, ws_ref[2 * p + 1], 0.0)],
            axis=0).astype(BF16)
        vcat = jnp.concatenate(
            [v[n * GM_BLOCK:(n + 1) * GM_BLOCK, p * LANES:(p + 1) * LANES] for n in range(nb)], axis=1)
        r = _dot(w2, vcat)
        zparts.append(jnp.where(first_head, r[:GM_BLOCK], r[GM_BLOCK:]))
    z = jnp.concatenate(
        [jnp.concatenate([zp[:, n * LANES:(n + 1) * LANES] for zp in zparts], axis=1) + gbias_ref[...]
         for n in range(nb)], axis=0)
    y_gm = _dot((pu * z).astype(BF16), gm_wo_ref[...])

    ca = _dot(hb, w_in_ref[:, C_CA:C_CA + CF_WIDTH])
    cg = _dot(hb, w_in_ref[:, C_CG:C_CG + CF_WIDTH])
    abuf[CF_TAIL:CF_TAIL + TM, :] = ca * jax.nn.sigmoid(cg)
    for r in range(1, SUBLANES):
        shbuf[r - 1, 0:TM + CF_TAIL - SUBLANES, :] = abuf[r:r + TM + CF_TAIL - SUBLANES, :]

    ngrp = CONV_ROWS // SUBLANES
    conv_rows = []
    for c in range(TM // CONV_ROWS):
        row0 = c * CONV_ROWS
        acc = [None] * ngrp
        for k in range(CF_KERNEL):
            s = k + CF_TAIL - (CF_KERNEL - 1)
            q, r = divmod(s, SUBLANES)
            wk = dww_ref[k * SUBLANES:(k + 1) * SUBLANES, :]
            for g in range(ngrp):
                start = row0 + (q + g) * SUBLANES
                if r == 0:
                    val = abuf[start:start + SUBLANES, :]
                else:
                    val = shbuf[r - 1, start:start + SUBLANES, :]
                acc[g] = wk * val if acc[g] is None else acc[g] + wk * val
        conv_rows.extend(acc)
    cv = jnp.concatenate(conv_rows, axis=0) + dwb_ref[...]
    mu = jnp.mean(cv, axis=-1, keepdims=True)
    cc = cv - mu
    var = jnp.mean(cc * cc, axis=-1, keepdims=True)
    ln = cc * lax.rsqrt(var + EPS) * lng_ref[...] + lnb_ref[...]
    y_cf = _dot(jax.nn.silu(ln).astype(BF16), cf_wo_ref[...])

    s_b = _dot(hb, w_in_ref[:, C_SB:C_SB + SC_WIDTH])
    s_c = _dot(hb, w_in_ref[:, C_SC:C_SC + SC_WIDTH])
    s_h = _dot(hb, w_in_ref[:, C_SH:C_SH + SC_WIDTH])
    sbuf[SC_TAIL:SC_TAIL + TM, :] = s_c * s_h
    conv3 = (scw_ref[0:1, :] * sbuf[SC_TAIL - 2:SC_TAIL - 2 + TM, :]
             + scw_ref[1:2, :] * sbuf[SC_TAIL - 1:SC_TAIL - 1 + TM, :]
             + scw_ref[2:3, :] * sbuf[SC_TAIL:SC_TAIL + TM, :])
    y_sc = _dot((s_b * conv3).astype(BF16), sc_wo_ref[...])

    m = jax.nn.sigmoid(_dot(hb, w_in_ref[:, C_GATES:C_GATES + D_MODEL])) * y_gm
    m = m + jax.nn.sigmoid(_dot(hb, w_in_ref[:, C_GATES + D_MODEL:C_GATES + 2 * D_MODEL])) * y_cf
    m = m + jax.nn.sigmoid(_dot(hb, w_in_ref[:, C_GATES + 2 * D_MODEL:C_GATES + 3 * D_MODEL])) * y_sc
    x2 = x + _dot(m.astype(BF16), w_o_ref[...])
    _store_token_rows(x2_ref, 0, x2)

    _, logits = _router_logits(x2, n2g_ref[...], wr_hi_ref[...], wr_lo_ref[...], rb_ref[...])
    lane = lax.broadcasted_iota(jnp.int32, (TM, LANES), 1).astype(F32)
    neg = jnp.float32(-jnp.inf)
    far = jnp.float32(1e9)
    is_g = lane < N_GROUPS
    gmax = jnp.max(jnp.where(is_g, logits, neg), axis=-1, keepdims=True)
    gidx = jnp.min(jnp.where(is_g & (logits == gmax), lane, far), axis=-1, keepdims=True)
    e_lo = N_GROUPS + EXPERTS_PER_GROUP * gidx
    in_grp = (lane >= e_lo) & (lane < e_lo + EXPERTS_PER_GROUP)
    el = jnp.where(in_grp, logits, neg)
    v1 = jnp.max(el, axis=-1, keepdims=True)
    i1 = jnp.min(jnp.where(in_grp & (logits == v1), lane, far), axis=-1, keepdims=True)
    rest = in_grp & (lane != i1)
    v2 = jnp.max(jnp.where(rest, logits, neg), axis=-1, keepdims=True)
    i2 = jnp.min(jnp.where(rest & (logits == v2), lane, far), axis=-1, keepdims=True)
    p_lo = jnp.minimum(i1, i2) - e_lo
    p_hi = jnp.maximum(i1, i2) - e_lo
    pair = p_lo * (7.0 - p_lo) * 0.5 + p_hi - p_lo - 1.0
    bucket = gidx * N_PAIRS + pair

    hit = lane == bucket
    onehot = hit.astype(BF16)
    row_id = lax.broadcasted_iota(jnp.int32, (TM, TM), 0)
    col_id = lax.broadcasted_iota(jnp.int32, (TM, TM), 1)
    prefix = _dot((row_id >= col_id).astype(BF16), onehot)
    rank = jnp.sum(jnp.where(hit, prefix - 1.0 + base[...], 0.0), axis=-1, keepdims=True)
    base[...] = base[...] + prefix[TM - 1:TM, :]
    packed = bucket * float(1 << RANK_BITS) + rank
    packed_row = jnp.sum(jnp.where(row_id == col_id, packed, 0.0), axis=0, keepdims=True)
    route_ref[...] = packed_row.astype(jnp.int32)
    cnt_ref[...] = base[...].astype(jnp.int32)


def _mixer_call(x, layer, p, tiles_per_seq, rows_in):
    n = p["n_tokens"]
    grid = (n // TM,)

    def const2(shape):
        return pl.BlockSpec(shape, lambda i: (0, 0))

    def layer2(shape):
        return pl.BlockSpec((None,) + shape, lambda i: (layer, 0, 0))

    def layer3(shape):
        return pl.BlockSpec((None,) + shape, lambda i: (layer, 0, 0, 0))

    x_spec = (pl.BlockSpec((TM * TOKEN_ROWS, LANES), lambda i: (i, 0)) if rows_in
              else pl.BlockSpec((TM, D_MODEL), lambda i: (i, 0)))
    in_specs = [
        x_spec,
        layer2((1, D_MODEL)),
        layer2((D_MODEL, W_IN_COLS)),
        layer2((1, GM_WIDTH)),
        const2((GM_WIDTH, GM_WIDTH)),
        layer3((GM_HEADS, GM_BLOCK, GM_BLOCK)),
        layer2((GM_BLOCK, GM_WIDTH)),
        layer2((GM_WIDTH, D_MODEL)),
        layer2((CF_KERNEL * SUBLANES, CF_WIDTH)),
        layer2((1, CF_WIDTH)),
        layer2((1, CF_WIDTH)),
        layer2((1, CF_WIDTH)),
        layer2((CF_WIDTH, D_MODEL)),
        layer2((SC_KERNEL, SC_WIDTH)),
        layer2((SC_WIDTH, D_MODEL)),
        layer2((D_MODEL, D_MODEL)),
        layer2((1, D_MODEL)),
        layer2((D_MODEL, LANES)),
        layer2((D_MODEL, LANES)),
        layer2((1, LANES)),
    ]
    out_specs = [
        pl.BlockSpec((TM * TOKEN_ROWS, LANES), lambda i: (i, 0)),
        pl.BlockSpec((None, 1, TM), lambda i: (i, 0, 0)),
        pl.BlockSpec((1, LANES), lambda i: (0, 0)),
    ]
    out_shape = [
        jax.ShapeDtypeStruct((n * TOKEN_ROWS, LANES), F32),
        jax.ShapeDtypeStruct((n // TM, 1, TM), jnp.int32),
        jax.ShapeDtypeStruct((1, LANES), jnp.int32),
    ]
    scratch = [
        pltpu.VMEM((TM + CF_TAIL, CF_WIDTH), F32),
        pltpu.VMEM((SUBLANES - 1, TM + CF_TAIL, CF_WIDTH), F32),
        pltpu.VMEM((TM + SC_TAIL, SC_WIDTH), F32),
        pltpu.VMEM((1, LANES), F32),
    ]
    return pl.pallas_call(
        functools.partial(_mixer_kernel, tiles_per_seq=tiles_per_seq, rows_in=rows_in),
        grid=grid, in_specs=in_specs, out_specs=out_specs, out_shape=out_shape,
        scratch_shapes=scratch,
        compiler_params=pltpu.CompilerParams(
            dimension_semantics=("arbitrary",), vmem_limit_bytes=VMEM_LIMIT),
        name=f"mixer_l{layer}",
    )(x, p["norm1_g"], p["w_in"], p["gm_vnorm_g"], p["gmat"], p["gm_ws"], p["gm_bias"], p["gm_w_out"],
      p["cf_dw_w"], p["cf_dw_b"], p["cf_ln_g"], p["cf_ln_b"], p["cf_w_out"], p["sc_conv_w"],
      p["sc_w_out"], p["w_o"], p["norm2_g"], p["wr_hi"], p["wr_lo"], p["rb"])


def _moe_kernel(tinfo_ref, nt_ref, tstart_ref, packed_ref,
                x2_hbm, n2g_ref, wr_hi_ref, wr_lo_ref, rb_ref,
                wga_ref, wua_ref, wda_ref, wgb_ref, wub_ref, wdb_ref, fg_ref,
                out_hbm,
                xbuf, obuf, src_s, gsem, ssem, *, final, n_tokens):
    t = pl.program_id(0)
    nt = nt_ref[0]
    slot = t % 2

    def valid_rows(tile):
        return tinfo_ref[4 * tile + 3]

    def buf_row(sl, r):
        return pl.multiple_of((sl * TR + r) * TOKEN_ROWS, TOKEN_ROWS)

    def token_rows(hbm, tile, r):
        tok = src_s[tile * TR + r]
        return hbm.at[pl.ds(pl.multiple_of(tok * TOKEN_ROWS, TOKEN_ROWS), TOKEN_ROWS), :]

    def gather_row(tile, sl, r):
        return pltpu.make_async_copy(
            token_rows(x2_hbm, tile, r), xbuf.at[pl.ds(buf_row(sl, r), TOKEN_ROWS), :], gsem.at[sl])

    def scatter_row(tile, sl, r):
        return pltpu.make_async_copy(
            obuf.at[pl.ds(buf_row(sl, r), TOKEN_ROWS), :], token_rows(out_hbm, tile, r), ssem.at[sl])

    def for_rows(count, fn):
        groups = lax.shift_right_logical(count, 3)

        def group(g, c):
            for u in range(DMA_UNROLL):
                fn(g * DMA_UNROLL + u)
            return c
        lax.fori_loop(0, groups, group, 0)

        def single(r, c):
            fn(r)
            return c
        lax.fori_loop(groups * DMA_UNROLL, count, single, 0)

    def start_rows(make, tile, sl):
        for_rows(valid_rows(tile), lambda r: make(tile, sl, r).start())

    def wait_rows(make, tile, sl):
        for_rows(valid_rows(tile), lambda r: make(tile, sl, r).wait())

    @pl.when(t == 0)
    def _():
        def fill(g, c):
            for u in range(DMA_UNROLL):
                n = g * DMA_UNROLL + u
                word = packed_ref[n]
                bkt = lax.shift_right_logical(word, RANK_BITS)
                src_s[tstart_ref[bkt] * TR + (word & ((1 << RANK_BITS) - 1))] = n
            return c
        lax.fori_loop(0, n_tokens // DMA_UNROLL, fill, 0)
        xbuf[...] = jnp.zeros(xbuf.shape, F32)

    @pl.when(t < nt)
    def _():
        @pl.when(t == 0)
        def _():
            start_rows(gather_row, t, slot)

        @pl.when(t + 1 < nt)
        def _():
            start_rows(gather_row, t + 1, 1 - slot)

        wait_rows(gather_row, t, slot)

        @pl.when(t >= 2)
        def _():
            wait_rows(scatter_row, t - 2, slot)

        grp = tinfo_ref[4 * t]
        ea = tinfo_ref[4 * t + 1]
        eb = tinfo_ref[4 * t + 2]
        xr = _load_token_rows(xbuf, slot * (TR * TOKEN_ROWS), TR)
        h2, logits = _router_logits(xr, n2g_ref[...], wr_hi_ref[...], wr_lo_ref[...], rb_ref[...])
        lane = lax.broadcasted_iota(jnp.int32, (TR, LANES), 1)
        is_g = lane < N_GROUPS
        gmax = jnp.max(jnp.where(is_g, logits, -jnp.inf), axis=-1, keepdims=True)
        eg = jnp.where(is_g, jnp.exp(logits - gmax), 0.0)
        p_top = (jnp.sum(jnp.where(lane == grp, eg, 0.0), axis=-1, keepdims=True)
                 / jnp.sum(eg, axis=-1, keepdims=True))
        la = jnp.sum(jnp.where(lane == N_GROUPS + ea, logits, 0.0), axis=-1, keepdims=True)
        lb = jnp.sum(jnp.where(lane == N_GROUPS + eb, logits, 0.0), axis=-1, keepdims=True)
        mx = jnp.maximum(la, lb)
        pa = jnp.exp(la - mx)
        pb = jnp.exp(lb - mx)
        wa = pa / (pa + pb) * p_top
        wb = pb / (pa + pb) * p_top

        hb = h2.astype(BF16)
        def w(ref):
            return ref[...].astype(BF16)

        act_a = jax.nn.silu(_dot(hb, w(wga_ref))) * _dot(hb, w(wua_ref))
        act_b = jax.nn.silu(_dot(hb, w(wgb_ref))) * _dot(hb, w(wub_ref))
        y = _dot((wa * act_a).astype(BF16), w(wda_ref)) + _dot((wb * act_b).astype(BF16), w(wdb_ref))
        res = xr + y
        if final:
            res = res * lax.rsqrt(jnp.mean(res * res, axis=-1, keepdims=True) + EPS) * fg_ref[...]
        _store_token_rows(obuf, slot * (TR * TOKEN_ROWS), res)

        start_rows(scatter_row, t, slot)

        @pl.when(t == nt - 1)
        def _():
            @pl.when(t >= 1)
            def _():
                wait_rows(scatter_row, t - 1, 1 - slot)
            wait_rows(scatter_row, t, slot)


def _moe_call(x2, layer, p, tinfo, nt, tstart, packed, final):
    n = p["n_tokens"]
    max_tiles = n // TR + N_BUCKETS

    def const2(shape):
        return pl.BlockSpec(shape, lambda t, *_: (0, 0))

    def layer2(shape):
        return pl.BlockSpec((None,) + shape, lambda t, *_: (layer, 0, 0))

    def expert(shape, which):
        def imap(t, tinfo_ref, nt_ref, tstart_ref, packed_ref):
            tt = jnp.minimum(t, nt_ref[0] - 1)
            return (layer, tinfo_ref[4 * tt + which], 0, 0)
        return pl.BlockSpec((None, None) + shape, imap)

    in_specs = [
        pl.BlockSpec(memory_space=pl.ANY),
        layer2((1, D_MODEL)),
        layer2((D_MODEL, LANES)),
        layer2((D_MODEL, LANES)),
        layer2((1, LANES)),
        expert((D_MODEL, D_EXPERT), 1),
        expert((D_MODEL, D_EXPERT), 1),
        expert((D_EXPERT, D_MODEL), 1),
        expert((D_MODEL, D_EXPERT), 2),
        expert((D_MODEL, D_EXPERT), 2),
        expert((D_EXPERT, D_MODEL), 2),
        const2((1, D_MODEL)),
    ]
    grid_spec = pltpu.PrefetchScalarGridSpec(
        num_scalar_prefetch=4, grid=(max_tiles,), in_specs=in_specs,
        out_specs=pl.BlockSpec(memory_space=pl.ANY),
        scratch_shapes=[
            pltpu.VMEM((2 * TR * TOKEN_ROWS, LANES), F32),
            pltpu.VMEM((2 * TR * TOKEN_ROWS, LANES), F32),
            pltpu.SMEM((max_tiles * TR,), jnp.int32),
            pltpu.SemaphoreType.DMA((2,)),
            pltpu.SemaphoreType.DMA((2,)),
        ])
    return pl.pallas_call(
        functools.partial(_moe_kernel, final=final, n_tokens=n),
        grid_spec=grid_spec,
        out_shape=jax.ShapeDtypeStruct((n * TOKEN_ROWS, LANES), F32),
        compiler_params=pltpu.CompilerParams(
            dimension_semantics=("arbitrary",), vmem_limit_bytes=VMEM_LIMIT),
        name=f"moe_l{layer}",
    )(tinfo, nt, tstart, packed, x2, p["norm2_g"], p["wr_hi"], p["wr_lo"], p["rb"],
      p["exp_w_gate"], p["exp_w_up"], p["exp_w_down"], p["exp_w_gate"], p["exp_w_up"], p["exp_w_down"],
      p["final_norm_g"])


def _tile_tables(counts, n):
    max_tiles = n // TR + N_BUCKETS
    order = np.asarray([g * N_PAIRS + q for g in range(N_GROUPS) for q in PAIR_ORDER], np.int32)
    cnt = counts[0, :N_BUCKETS][order]
    tiles = (cnt + TR - 1) // TR
    tile_end = jnp.cumsum(tiles)
    tile_start = tile_end - tiles
    nt = tile_end[-1:].astype(jnp.int32)
    tile_ids = jnp.arange(max_tiles, dtype=jnp.int32)
    tpos = jnp.sum((tile_ids[:, None] >= tile_end[None, :]).astype(jnp.int32), axis=1)
    tpos = jnp.minimum(tpos, N_BUCKETS - 1)
    grp = jnp.asarray(order // N_PAIRS)[tpos]
    pr = jnp.asarray(order % N_PAIRS)[tpos]
    ea = grp * EXPERTS_PER_GROUP + jnp.asarray(PAIR_A, jnp.int32)[pr]
    eb = grp * EXPERTS_PER_GROUP + jnp.asarray(PAIR_B, jnp.int32)[pr]
    valid = jnp.clip(cnt[tpos] - (tile_ids - tile_start[tpos]) * TR, 0, TR)
    tinfo = jnp.stack([grp, ea, eb, valid], axis=1).reshape(-1).astype(jnp.int32)
    tstart = tile_start.astype(jnp.int32)[np.argsort(order)]
    return tinfo, nt, tstart


def kernel(x, norm1_g, w_in, gm_vnorm_g, gm_ws, gm_bs, gm_w_out, cf_dw_w, cf_dw_b, cf_ln_g, cf_ln_b,
           cf_w_out, sc_conv_w, sc_w_out, w_o, norm2_g, router_g, router_g_b, router_e, router_e_b,
           exp_w_gate, exp_w_up, exp_w_down, final_norm_g):
    b, s, d = x.shape
    depth = w_in.shape[0]
    n = b * s
    assert d == D_MODEL and s % TM == 0 and TM % GM_BLOCK == 0 and n % TR == 0
    assert n <= (1 << RANK_BITS) and TR % DMA_UNROLL == 0 and n % DMA_UNROLL == 0

    wr = jnp.concatenate([router_g, router_e], axis=-1)
    wr = jnp.pad(wr, ((0, 0), (0, 0), (0, LANES - wr.shape[-1])))
    wr_hi = wr.astype(BF16)
    wr_lo = (wr - wr_hi.astype(F32)).astype(BF16)
    rb = jnp.concatenate([router_g_b, router_e_b], axis=-1)
    rb = jnp.pad(rb, ((0, 0), (0, LANES - rb.shape[-1])))[:, None, :]
    head = np.arange(GM_WIDTH) // GM_HEAD_DIM
    gmat = jnp.asarray((head[:, None] == head[None, :]).astype(np.float32) / GM_HEAD_DIM, BF16)

    p = dict(
        n_tokens=n,
        norm1_g=norm1_g[:, None, :],
        w_in=w_in.astype(BF16),
        gm_vnorm_g=gm_vnorm_g[:, None, :],
        gmat=gmat,
        gm_ws=gm_ws,
        gm_bias=jnp.repeat(jnp.swapaxes(gm_bs, 1, 2), GM_HEAD_DIM, axis=2),
        gm_w_out=gm_w_out.astype(BF16),
        cf_dw_w=jnp.repeat(cf_dw_w, SUBLANES, axis=1),
        cf_dw_b=cf_dw_b[:, None, :],
        cf_ln_g=cf_ln_g[:, None, :],
        cf_ln_b=cf_ln_b[:, None, :],
        cf_w_out=cf_w_out.astype(BF16),
        sc_conv_w=sc_conv_w,
        sc_w_out=sc_w_out.astype(BF16),
        w_o=w_o.astype(BF16),
        norm2_g=norm2_g[:, None, :],
        wr_hi=wr_hi, wr_lo=wr_lo, rb=rb,
        exp_w_gate=exp_w_gate,
        exp_w_up=exp_w_up,
        exp_w_down=exp_w_down,
        final_norm_g=final_norm_g[None, :],
    )

    xf = x.reshape(n, d)
    for layer in range(depth):
        x2, route, counts = _mixer_call(xf, layer, p, s // TM, rows_in=(layer > 0))
        tinfo, nt, tstart = _tile_tables(counts, n)
        xf = _moe_call(x2, layer, p, tinfo, nt, tstart, route.reshape(n), final=(layer == depth - 1))
    return xf.reshape(b, s, d)
```

```python
import functools

import numpy as np
import jax
import jax.numpy as jnp
from jax import lax
from jax.experimental import pallas as pl
from jax.experimental.pallas import tpu as pltpu

F32 = jnp.float32
BF16 = jnp.bfloat16

D_MODEL = 1024
CHUNK = 64
GM_BLOCK = 128
GM_HEADS = 8
GM_WIDTH = 512
GM_HEAD_DIM = GM_WIDTH // GM_HEADS
CF_WIDTH = 512
CF_KERNEL = 31
SC_WIDTH = 512
SC_KERNEL = 3
N_BRANCH = 3
N_GROUPS = 4
EXPERTS_PER_GROUP = 4
N_EXPERTS = N_GROUPS * EXPERTS_PER_GROUP
D_EXPERT = 512
EPS = 1e-6

C_GU, C_GV, C_CA, C_CG, C_SB, C_SC, C_SH, C_GATES = 0, 512, 1024, 1536, 2048, 2560, 3072, 3584
W_IN_COLS = C_GATES + N_BRANCH * D_MODEL

LANES = 128
SUBLANES = 8
TOKEN_ROWS = D_MODEL // LANES
N_PAIRS = 6
N_BUCKETS = N_GROUPS * N_PAIRS
PAIR_A = (0, 0, 0, 1, 1, 3)
PAIR_B = (1, 2, 3, 2, 3, 2)
PAIR_ORDER = (0, 1, 2, 4, 3, 5)
RANK_BITS = 16
ROUTE_ROWS = 32

TM = 256
TR = 256
CF_TAIL = 32
SC_TAIL = 8
CONV_ROWS = 32
SLAB_STRIDE = 2
DMA_UNROLL = 8
VMEM_LIMIT = 56 * 1024 * 1024


def _dot(a, b):
    return jnp.dot(a, b, preferred_element_type=F32)


def _dot_nt(a, b):
    return lax.dot_general(a, b, (((1,), (1,)), ((), ())), preferred_element_type=F32)


def _split_bf16(v):
    hi = v.astype(BF16)
    lo = (v - hi.astype(F32)).astype(BF16)
    return hi, lo


def _router_logits(x2, n2g, wr_cat, rb):
    h2 = x2 * lax.rsqrt(jnp.mean(x2 * x2, axis=-1, keepdims=True) + EPS) * n2g
    hi, lo = _split_bf16(h2)
    both = _dot(hi, wr_cat)
    logits = both[:, :LANES] + both[:, LANES:] + _dot(lo, wr_cat[:, :LANES]) + rb
    return h2, logits


def _load_token_rows(ref, row0, n):
    return jnp.concatenate(
        [ref[pl.ds(row0 + s, n, stride=TOKEN_ROWS), :] for s in range(TOKEN_ROWS)], axis=1)


def _store_token_rows(ref, row0, val):
    n = val.shape[0]
    for s in range(TOKEN_ROWS):
        ref[pl.ds(row0 + s, n, stride=TOKEN_ROWS), :] = val[:, s * LANES:(s + 1) * LANES]


def _mixer_kernel(x_ref, n1g_ref, w_in_ref, vng_ref, gmat_ref, ws_ref, gbias_ref, gm_wo_ref,
                  dww_ref, dwb_ref, lng_ref, lnb_ref, cf_wo_ref, scw_ref, sc_wo_ref, w_o_ref,
                  n2g_ref, wrt_ref, rbt_ref,
                  x2_ref, route_ref, cnt_ref,
                  aslab, sslab, base, *, tiles_per_seq, rows_in):
    i = pl.program_id(0)
    first = (i % tiles_per_seq) == 0
    nb = TM // GM_BLOCK
    lane_groups = CF_WIDTH // LANES

    class Slab:
        def __init__(self, ref, hist_rows):
            self.ref, self.hist_rows = ref, hist_rows

        def _idx(self, group, row, n):
            start = SLAB_STRIDE * (group * self.hist_rows + row)
            return (pl.ds(start, n, stride=SLAB_STRIDE), slice(None))

        def load(self, group, row, n):
            return self.ref[self._idx(group, row, n)]

        def store(self, group, row, val):
            self.ref[self._idx(group, row, val.shape[0])] = val

    a_hist = Slab(aslab, TM + CF_TAIL)
    s_hist = Slab(sslab, TM + SC_TAIL)

    @pl.when(first)
    def _():
        for g in range(lane_groups):
            a_hist.store(g, 0, jnp.zeros((CF_TAIL, LANES), F32))
            s_hist.store(g, 0, jnp.zeros((SC_TAIL, LANES), F32))

    @pl.when(jnp.logical_not(first))
    def _():
        for g in range(lane_groups):
            a_hist.store(g, 0, a_hist.load(g, TM, CF_TAIL))
            s_hist.store(g, 0, s_hist.load(g, TM, SC_TAIL))

    @pl.when(i == 0)
    def _():
        base[...] = jnp.zeros(base.shape, F32)

    x = _load_token_rows(x_ref, 0, TM) if rows_in else x_ref[...]
    h = x * lax.rsqrt(jnp.mean(x * x, axis=-1, keepdims=True) + EPS) * n1g_ref[...]
    hb = h.astype(BF16)

    def proj(c0, width):
        return _dot(hb, w_in_ref[:, c0:c0 + width])

    glu = proj(C_CA, CF_WIDTH) * jax.nn.sigmoid(proj(C_CG, CF_WIDTH))
    for g in range(lane_groups):
        a_hist.store(g, CF_TAIL, glu[:, g * LANES:(g + 1) * LANES])
    pu, pv = proj(C_GU, GM_WIDTH), proj(C_GV, GM_WIDTH)
    s_b, s_c, s_h = proj(C_SB, SC_WIDTH), proj(C_SC, SC_WIDTH), proj(C_SH, SC_WIDTH)

    ngrp = CONV_ROWS // SUBLANES
    conv_rows = []
    for c in range(TM // CONV_ROWS):
        acc = [[None] * lane_groups for _ in range(ngrp)]
        for k in range(CF_KERNEL):
            shift = k + CF_TAIL - (CF_KERNEL - 1)
            for g in range(lane_groups):
                wk = dww_ref[k * SUBLANES:(k + 1) * SUBLANES, g * LANES:(g + 1) * LANES]
                for r in range(ngrp):
                    val = a_hist.load(g, c * CONV_ROWS + r * SUBLANES + shift, SUBLANES)
                    acc[r][g] = wk * val if acc[r][g] is None else acc[r][g] + wk * val
        conv_rows.extend(jnp.concatenate(a, axis=1) for a in acc)

    sq_hi, sq_lo = _split_bf16(pv * pv)
    ms = _dot(sq_hi, gmat_ref[...]) + _dot(sq_lo, gmat_ref[...])
    cv = jnp.concatenate(conv_rows, axis=0) + dwb_ref[...]
    mu = jnp.mean(cv, axis=-1, keepdims=True)
    cc = cv - mu
    var = jnp.mean(cc * cc, axis=-1, keepdims=True)
    sc_in = s_c * s_h
    conv3 = []
    for g in range(lane_groups):
        s_hist.store(g, SC_TAIL, sc_in[:, g * LANES:(g + 1) * LANES])
        taps = [scw_ref[k:k + 1, g * LANES:(g + 1) * LANES] * s_hist.load(g, SC_TAIL - (SC_KERNEL - 1) + k, TM)
                for k in range(SC_KERNEL)]
        conv3.append((taps[0] + taps[1]) + taps[2])
    y_sc = _dot((s_b * jnp.concatenate(conv3, axis=1)).astype(BF16), sc_wo_ref[...])
    g_gm, g_cf, g_sc = (proj(C_GATES + j * D_MODEL, D_MODEL) for j in range(N_BRANCH))
    ln = cc * lax.rsqrt(var + EPS) * lng_ref[...] + lnb_ref[...]
    y_cf = _dot(jax.nn.silu(ln).astype(BF16), cf_wo_ref[...])

    v = (pv * lax.rsqrt(ms + EPS) * vng_ref[...]).astype(BF16)
    ri = lax.broadcasted_iota(jnp.int32, (GM_BLOCK, GM_BLOCK), 0) // CHUNK
    ci = lax.broadcasted_iota(jnp.int32, (GM_BLOCK, GM_BLOCK), 1) // CHUNK
    causal = ci <= ri
    lane_in_pair = lax.broadcasted_iota(jnp.int32, (GM_BLOCK, nb * LANES), 1) % LANES
    first_head = lane_in_pair < GM_HEAD_DIM
    zparts = []
    for p in range(GM_HEADS // 2):
        w2 = jnp.concatenate(
            [jnp.where(causal, ws_ref[2 * p], 0.0), jnp.where(causal, ws_ref[2 * p + 1], 0.0)],
            axis=0).astype(BF16)
        vcat = jnp.concatenate(
            [v[n * GM_BLOCK:(n + 1) * GM_BLOCK, p * LANES:(p + 1) * LANES] for n in range(nb)], axis=1)
        r = _dot(w2, vcat)
        zparts.append(jnp.where(first_head, r[:GM_BLOCK], r[GM_BLOCK:]))
    z = jnp.concatenate(
        [jnp.concatenate([zp[:, n * LANES:(n + 1) * LANES] for zp in zparts], axis=1) + gbias_ref[...]
         for n in range(nb)], axis=0)
    y_gm = _dot((pu * z).astype(BF16), gm_wo_ref[...])

    m = (jax.nn.sigmoid(g_gm) * y_gm + jax.nn.sigmoid(g_cf) * y_cf) + jax.nn.sigmoid(g_sc) * y_sc
    x2 = x + _dot(m.astype(BF16), w_o_ref[...])
    _store_token_rows(x2_ref, 0, x2)

    h2 = x2 * lax.rsqrt(jnp.mean(x2 * x2, axis=-1, keepdims=True) + EPS) * n2g_ref[...]
    h_hi, h_lo = _split_bf16(h2)
    by_hi = _dot_nt(wrt_ref[...], h_hi)
    by_lo = _dot_nt(wrt_ref[0:ROUTE_ROWS, :], h_lo)
    lt = (by_hi[:ROUTE_ROWS] + by_hi[ROUTE_ROWS:]) + by_lo + rbt_ref[:, 0:1]
    row = [lt[j:j + 1, :] for j in range(N_GROUPS + N_EXPERTS)]

    def first_argmax(vals, skip=None):
        best_v = jnp.full_like(vals[0], -jnp.inf)
        best_i = jnp.zeros_like(vals[0])
        for j, vj in enumerate(vals):
            better = vj > best_v
            if skip is not None:
                better = better & (skip != float(j))
            best_i = jnp.where(better, float(j), best_i)
            best_v = jnp.where(better, vj, best_v)
        return best_i, best_v

    gidx, _ = first_argmax(row[:N_GROUPS])
    in_group = [
        jnp.where(gidx == 0.0, row[N_GROUPS + j],
                  jnp.where(gidx == 1.0, row[N_GROUPS + EXPERTS_PER_GROUP + j],
                            jnp.where(gidx == 2.0, row[N_GROUPS + 2 * EXPERTS_PER_GROUP + j],
                                      row[N_GROUPS + 3 * EXPERTS_PER_GROUP + j])))
        for j in range(EXPERTS_PER_GROUP)]
    i1, _ = first_argmax(in_group)
    i2, _ = first_argmax(in_group, skip=i1)
    p_lo = jnp.minimum(i1, i2)
    p_hi = jnp.maximum(i1, i2)
    pair = p_lo * (7.0 - p_lo) * 0.5 + p_hi - p_lo - 1.0
    bucket = gidx * N_PAIRS + pair

    hit = lax.broadcasted_iota(jnp.int32, (ROUTE_ROWS, TM), 0).astype(F32) == bucket
    onehot = hit.astype(BF16)
    upper = (lax.broadcasted_iota(jnp.int32, (TM, TM), 0)
             <= lax.broadcasted_iota(jnp.int32, (TM, TM), 1)).astype(BF16)
    prefix = _dot(onehot, upper)
    rank = jnp.sum(jnp.where(hit, prefix - 1.0 + base[:, 0:1], 0.0), axis=0, keepdims=True)
    base[...] = base[...] + jnp.sum(hit.astype(F32), axis=1, keepdims=True)
    packed = bucket * float(1 << RANK_BITS) + rank
    route_ref[...] = packed.astype(jnp.int32)
    cnt_ref[...] = base[...].astype(jnp.int32)


def _mixer_call(x, layer, p, tiles_per_seq, rows_in):
    n = p["n_tokens"]
    grid = (n // TM,)

    def const2(shape):
        return pl.BlockSpec(shape, lambda i: (0, 0))

    def layer2(shape):
        return pl.BlockSpec((None,) + shape, lambda i: (layer, 0, 0))

    def layer3(shape):
        return pl.BlockSpec((None,) + shape, lambda i: (layer, 0, 0, 0))

    x_spec = (pl.BlockSpec((TM * TOKEN_ROWS, LANES), lambda i: (i, 0)) if rows_in
              else pl.BlockSpec((TM, D_MODEL), lambda i: (i, 0)))
    in_specs = [
        x_spec,
        layer2((1, D_MODEL)),
        layer2((D_MODEL, W_IN_COLS)),
        layer2((1, GM_WIDTH)),
        const2((GM_WIDTH, GM_WIDTH)),
        layer3((GM_HEADS, GM_BLOCK, GM_BLOCK)),
        layer2((GM_BLOCK, GM_WIDTH)),
        layer2((GM_WIDTH, D_MODEL)),
        layer2((CF_KERNEL * SUBLANES, CF_WIDTH)),
        layer2((1, CF_WIDTH)),
        layer2((1, CF_WIDTH)),
        layer2((1, CF_WIDTH)),
        layer2((CF_WIDTH, D_MODEL)),
        layer2((SC_KERNEL, SC_WIDTH)),
        layer2((SC_WIDTH, D_MODEL)),
        layer2((D_MODEL, D_MODEL)),
        layer2((1, D_MODEL)),
        layer2((2 * ROUTE_ROWS, D_MODEL)),
        layer2((ROUTE_ROWS, LANES)),
    ]
    out_specs = [
        pl.BlockSpec((TM * TOKEN_ROWS, LANES), lambda i: (i, 0)),
        pl.BlockSpec((None, 1, TM), lambda i: (i, 0, 0)),
        pl.BlockSpec((ROUTE_ROWS, LANES), lambda i: (0, 0)),
    ]
    out_shape = [
        jax.ShapeDtypeStruct((n * TOKEN_ROWS, LANES), F32),
        jax.ShapeDtypeStruct((n // TM, 1, TM), jnp.int32),
        jax.ShapeDtypeStruct((ROUTE_ROWS, LANES), jnp.int32),
    ]
    lane_groups = CF_WIDTH // LANES
    scratch = [
        pltpu.VMEM((lane_groups * SLAB_STRIDE * (TM + CF_TAIL), LANES), F32),
        pltpu.VMEM((lane_groups * SLAB_STRIDE * (TM + SC_TAIL), LANES), F32),
        pltpu.VMEM((ROUTE_ROWS, LANES), F32),
    ]
    return pl.pallas_call(
        functools.partial(_mixer_kernel, tiles_per_seq=tiles_per_seq, rows_in=rows_in),
        grid=grid, in_specs=in_specs, out_specs=out_specs, out_shape=out_shape,
        scratch_shapes=scratch,
        compiler_params=pltpu.CompilerParams(
            dimension_semantics=("arbitrary",), vmem_limit_bytes=VMEM_LIMIT),
        name=f"mixer_l{layer}",
    )(x, p["norm1_g"], p["w_in"], p["gm_vnorm_g"], p["gmat"], p["gm_ws"], p["gm_bias"], p["gm_w_out"],
      p["cf_dw_w"], p["cf_dw_b"], p["cf_ln_g"], p["cf_ln_b"], p["cf_w_out"], p["sc_conv_w"],
      p["sc_w_out"], p["w_o"], p["norm2_g"], p["wrt"], p["rbt"])


def _moe_kernel(tinfo_ref, nt_ref, tstart_ref, packed_ref,
                x2_hbm, n2g_ref, wr_cat_ref, rb_ref,
                wga_ref, wua_ref, wda_ref, wgb_ref, wub_ref, wdb_ref, fg_ref,
                out_hbm,
                xbuf, obuf, src_s, gsem, ssem, *, final, n_tokens):
    t = pl.program_id(0)
    nt = nt_ref[0]
    slot = t % 2

    def valid_rows(tile):
        return tinfo_ref[4 * tile + 3]

    def buf_row(sl, r):
        return pl.multiple_of((sl * TR + r) * TOKEN_ROWS, TOKEN_ROWS)

    def token_rows(hbm, tile, r):
        tok = src_s[tile * TR + r]
        return hbm.at[pl.ds(pl.multiple_of(tok * TOKEN_ROWS, TOKEN_ROWS), TOKEN_ROWS), :]

    def gather_row(tile, sl, r):
        return pltpu.make_async_copy(
            token_rows(x2_hbm, tile, r), xbuf.at[pl.ds(buf_row(sl, r), TOKEN_ROWS), :], gsem.at[sl])

    def scatter_row(tile, sl, r):
        return pltpu.make_async_copy(
            obuf.at[pl.ds(buf_row(sl, r), TOKEN_ROWS), :], token_rows(out_hbm, tile, r), ssem.at[sl])

    def for_rows(count, fn):
        groups = lax.shift_right_logical(count, 3)

        def group(g, c):
            for u in range(DMA_UNROLL):
                fn(g * DMA_UNROLL + u)
            return c
        lax.fori_loop(0, groups, group, 0)

        def single(r, c):
            fn(r)
            return c
        lax.fori_loop(groups * DMA_UNROLL, count, single, 0)

    def start_rows(make, tile, sl):
        for_rows(valid_rows(tile), lambda r: make(tile, sl, r).start())

    def wait_rows(make, tile, sl):
        for_rows(valid_rows(tile), lambda r: make(tile, sl, r).wait())

    @pl.when(t == 0)
    def _():
        def fill(g, c):
            for u in range(DMA_UNROLL):
                n = g * DMA_UNROLL + u
                word = packed_ref[n]
                bkt = lax.shift_right_logical(word, RANK_BITS)
                src_s[tstart_ref[bkt] * TR + (word & ((1 << RANK_BITS) - 1))] = n
            return c
        lax.fori_loop(0, n_tokens // DMA_UNROLL, fill, 0)
        xbuf[...] = jnp.zeros(xbuf.shape, F32)

    @pl.when(t < nt)
    def _():
        @pl.when(t == 0)
        def _():
            start_rows(gather_row, t, slot)

        @pl.when(t + 1 < nt)
        def _():
            start_rows(gather_row, t + 1, 1 - slot)

        wait_rows(gather_row, t, slot)

        @pl.when(t >= 2)
        def _():
            wait_rows(scatter_row, t - 2, slot)

        grp = tinfo_ref[4 * t]
        ea = tinfo_ref[4 * t + 1]
        eb = tinfo_ref[4 * t + 2]
        xr = _load_token_rows(xbuf, slot * (TR * TOKEN_ROWS), TR)
        h2, logits = _router_logits(xr, n2g_ref[...], wr_cat_ref[...], rb_ref[...])
        lane = lax.broadcasted_iota(jnp.int32, (TR, LANES), 1)
        is_g = lane < N_GROUPS
        gmax = jnp.max(jnp.where(is_g, logits, -jnp.inf), axis=-1, keepdims=True)
        eg = jnp.where(is_g, jnp.exp(logits - gmax), 0.0)
        p_top = (jnp.sum(jnp.where(lane == grp, eg, 0.0), axis=-1, keepdims=True)
                 / jnp.sum(eg, axis=-1, keepdims=True))
        la = jnp.sum(jnp.where(lane == N_GROUPS + ea, logits, 0.0), axis=-1, keepdims=True)
        lb = jnp.sum(jnp.where(lane == N_GROUPS + eb, logits, 0.0), axis=-1, keepdims=True)
        mx = jnp.maximum(la, lb)
        pa = jnp.exp(la - mx)
        pb = jnp.exp(lb - mx)
        wa = pa / (pa + pb) * p_top
        wb = pb / (pa + pb) * p_top

        hb = h2.astype(BF16)
        def w(ref):
            return ref[...].astype(BF16)

        act_a = jax.nn.silu(_dot(hb, w(wga_ref))) * _dot(hb, w(wua_ref))
        act_b = jax.nn.silu(_dot(hb, w(wgb_ref))) * _dot(hb, w(wub_ref))
        y = _dot((wa * act_a).astype(BF16), w(wda_ref)) + _dot((wb * act_b).astype(BF16), w(wdb_ref))
        res = xr + y
        if final:
            res = res * lax.rsqrt(jnp.mean(res * res, axis=-1, keepdims=True) + EPS) * fg_ref[...]
        _store_token_rows(obuf, slot * (TR * TOKEN_ROWS), res)

        start_rows(scatter_row, t, slot)

        @pl.when(t == nt - 1)
        def _():
            @pl.when(t >= 1)
            def _():
                wait_rows(scatter_row, t - 1, 1 - slot)
            wait_rows(scatter_row, t, slot)


def _moe_call(x2, layer, p, tinfo, nt, tstart, packed, final):
    n = p["n_tokens"]
    max_tiles = n // TR + N_BUCKETS

    def const2(shape):
        return pl.BlockSpec(shape, lambda t, *_: (0, 0))

    def layer2(shape):
        return pl.BlockSpec((None,) + shape, lambda t, *_: (layer, 0, 0))

    def expert(shape, which):
        def imap(t, tinfo_ref, nt_ref, tstart_ref, packed_ref):
            tt = jnp.minimum(t, nt_ref[0] - 1)
            return (layer, tinfo_ref[4 * tt + which], 0, 0)
        return pl.BlockSpec((None, None) + shape, imap)

    in_specs = [
        pl.BlockSpec(memory_space=pl.ANY),
        layer2((1, D_MODEL)),
        layer2((D_MODEL, 2 * LANES)),
        layer2((1, LANES)),
        expert((D_MODEL, D_EXPERT), 1),
        expert((D_MODEL, D_EXPERT), 1),
        expert((D_EXPERT, D_MODEL), 1),
        expert((D_MODEL, D_EXPERT), 2),
        expert((D_MODEL, D_EXPERT), 2),
        expert((D_EXPERT, D_MODEL), 2),
        const2((1, D_MODEL)),
    ]
    grid_spec = pltpu.PrefetchScalarGridSpec(
        num_scalar_prefetch=4, grid=(max_tiles,), in_specs=in_specs,
        out_specs=pl.BlockSpec(memory_space=pl.ANY),
        scratch_shapes=[
            pltpu.VMEM((2 * TR * TOKEN_ROWS, LANES), F32),
            pltpu.VMEM((2 * TR * TOKEN_ROWS, LANES), F32),
            pltpu.SMEM((max_tiles * TR,), jnp.int32),
            pltpu.SemaphoreType.DMA((2,)),
            pltpu.SemaphoreType.DMA((2,)),
        ])
    return pl.pallas_call(
        functools.partial(_moe_kernel, final=final, n_tokens=n),
        grid_spec=grid_spec,
        out_shape=jax.ShapeDtypeStruct((n * TOKEN_ROWS, LANES), F32),
        compiler_params=pltpu.CompilerParams(
            dimension_semantics=("arbitrary",), vmem_limit_bytes=VMEM_LIMIT),
        name=f"moe_l{layer}",
    )(tinfo, nt, tstart, packed, x2, p["norm2_g"], p["wr_cat"], p["rb"],
      p["exp_w_gate"], p["exp_w_up"], p["exp_w_down"], p["exp_w_gate"], p["exp_w_up"], p["exp_w_down"],
      p["final_norm_g"])


def _tile_tables(counts, n):
    max_tiles = n // TR + N_BUCKETS
    order = np.asarray([g * N_PAIRS + q for g in range(N_GROUPS) for q in PAIR_ORDER], np.int32)
    cnt = counts[:N_BUCKETS, 0][order]
    tiles = (cnt + TR - 1) // TR
    tile_end = jnp.cumsum(tiles)
    tile_start = tile_end - tiles
    nt = tile_end[-1:].astype(jnp.int32)
    tile_ids = jnp.arange(max_tiles, dtype=jnp.int32)
    tpos = jnp.sum((tile_ids[:, None] >= tile_end[None, :]).astype(jnp.int32), axis=1)
    tpos = jnp.minimum(tpos, N_BUCKETS - 1)
    grp = jnp.asarray(order // N_PAIRS)[tpos]
    pr = jnp.asarray(order % N_PAIRS)[tpos]
    ea = grp * EXPERTS_PER_GROUP + jnp.asarray(PAIR_A, jnp.int32)[pr]
    eb = grp * EXPERTS_PER_GROUP + jnp.asarray(PAIR_B, jnp.int32)[pr]
    valid = jnp.clip(cnt[tpos] - (tile_ids - tile_start[tpos]) * TR, 0, TR)
    tinfo = jnp.stack([grp, ea, eb, valid], axis=1).reshape(-1).astype(jnp.int32)
    tstart = tile_start.astype(jnp.int32)[np.argsort(order)]
    return tinfo, nt, tstart


def kernel(x, norm1_g, w_in, gm_vnorm_g, gm_ws, gm_bs, gm_w_out, cf_dw_w, cf_dw_b, cf_ln_g, cf_ln_b,
           cf_w_out, sc_conv_w, sc_w_out, w_o, norm2_g, router_g, router_g_b, router_e, router_e_b,
           exp_w_gate, exp_w_up, exp_w_down, final_norm_g):
    b, s, d = x.shape
    depth = w_in.shape[0]
    n = b * s
    assert d == D_MODEL and s % TM == 0 and TM % GM_BLOCK == 0 and n % TR == 0
    assert n <= (1 << RANK_BITS) and TR % DMA_UNROLL == 0 and n % DMA_UNROLL == 0

    wr = jnp.concatenate([router_g, router_e], axis=-1)
    wr = jnp.pad(wr, ((0, 0), (0, 0), (0, LANES - wr.shape[-1])))
    wr_hi = wr.astype(BF16)
    wr_lo = (wr - wr_hi.astype(F32)).astype(BF16)
    rb = jnp.concatenate([router_g_b, router_e_b], axis=-1)
    rbt = jnp.broadcast_to(jnp.pad(rb, ((0, 0), (0, ROUTE_ROWS - rb.shape[-1])))[:, :, None],
                           (depth, ROUTE_ROWS, LANES))
    rb = jnp.pad(rb, ((0, 0), (0, LANES - rb.shape[-1])))[:, None, :]
    wrt = jnp.swapaxes(jnp.concatenate([wr_hi[:, :, :ROUTE_ROWS], wr_lo[:, :, :ROUTE_ROWS]], axis=-1), 1, 2)
    head = np.arange(GM_WIDTH) // GM_HEAD_DIM
    gmat = jnp.asarray((head[:, None] == head[None, :]).astype(np.float32) / GM_HEAD_DIM, BF16)

    p = dict(
        n_tokens=n,
        norm1_g=norm1_g[:, None, :],
        w_in=w_in.astype(BF16),
        gm_vnorm_g=gm_vnorm_g[:, None, :],
        gmat=gmat,
        gm_ws=gm_ws,
        gm_bias=jnp.repeat(jnp.swapaxes(gm_bs, 1, 2), GM_HEAD_DIM, axis=2),
        gm_w_out=gm_w_out.astype(BF16),
        cf_dw_w=jnp.repeat(cf_dw_w, SUBLANES, axis=1),
        cf_dw_b=cf_dw_b[:, None, :],
        cf_ln_g=cf_ln_g[:, None, :],
        cf_ln_b=cf_ln_b[:, None, :],
        cf_w_out=cf_w_out.astype(BF16),
        sc_conv_w=sc_conv_w,
        sc_w_out=sc_w_out.astype(BF16),
        w_o=w_o.astype(BF16),
        norm2_g=norm2_g[:, None, :],
        wrt=wrt, rbt=rbt,
        wr_cat=jnp.concatenate([wr_hi, wr_lo], axis=-1), rb=rb,
        exp_w_gate=exp_w_gate,
        exp_w_up=exp_w_up,
        exp_w_down=exp_w_down,
        final_norm_g=final_norm_g[None, :],
    )

    xf = x.reshape(n, d)
    for layer in range(depth):
        x2, route, counts = _mixer_call(xf, layer, p, s // TM, rows_in=(layer > 0))
        tinfo, nt, tstart = _tile_tables(counts, n)
        xf = _moe_call(x2, layer, p, tinfo, nt, tstart, route.reshape(n), final=(layer == depth - 1))
    return xf.reshape(b, s, d)
```

```python
import functools

import numpy as np
import jax
import jax.numpy as jnp
from jax import lax
from jax.experimental import pallas as pl
from jax.experimental.pallas import tpu as pltpu

F32 = jnp.float32
BF16 = jnp.bfloat16

D_MODEL = 1024
CHUNK = 64
GM_BLOCK = 128
GM_HEADS = 8
GM_WIDTH = 512
GM_HEAD_DIM = GM_WIDTH // GM_HEADS
CF_WIDTH = 512
CF_KERNEL = 31
SC_WIDTH = 512
SC_KERNEL = 3
N_BRANCH = 3
N_GROUPS = 4
EXPERTS_PER_GROUP = 4
N_EXPERTS = N_GROUPS * EXPERTS_PER_GROUP
D_EXPERT = 512
EPS = 1e-6

C_GU, C_GV, C_CA, C_CG, C_SB, C_SC, C_SH, C_GATES = 0, 512, 1024, 1536, 2048, 2560, 3072, 3584
W_IN_COLS = C_GATES + N_BRANCH * D_MODEL

LANES = 128
SUBLANES = 8
TOKEN_ROWS = D_MODEL // LANES
N_PAIRS = 6
N_BUCKETS = N_GROUPS * N_PAIRS
PAIR_A = (0, 0, 0, 1, 1, 3)
PAIR_B = (1, 2, 3, 2, 3, 2)
PAIR_ORDER = (0, 1, 2, 4, 3, 5)
RANK_BITS = 16
ROUTE_ROWS = 32

TM = 256
TR = 256
CF_TAIL = 32
SC_TAIL = 8
CONV_ROWS = 32
SLAB_STRIDE = 2
DMA_UNROLL = 16
VMEM_LIMIT = 56 * 1024 * 1024


def _dot(a, b):
    return jnp.dot(a, b, preferred_element_type=F32)


def _dot_nt(a, b):
    return lax.dot_general(a, b, (((1,), (1,)), ((), ())), preferred_element_type=F32)


def _split_bf16(v):
    hi = v.astype(BF16)
    lo = (v - hi.astype(F32)).astype(BF16)
    return hi, lo


def _router_logits(x2, n2g, wr_cat, rb):
    h2 = x2 * lax.rsqrt(jnp.mean(x2 * x2, axis=-1, keepdims=True) + EPS) * n2g
    hi, lo = _split_bf16(h2)
    both = _dot(hi, wr_cat)
    logits = both[:, :LANES] + both[:, LANES:] + _dot(lo, wr_cat[:, :LANES]) + rb
    return h2, logits


def _load_token_rows(ref, row0, n):
    return jnp.concatenate(
        [ref[pl.ds(row0 + s, n, stride=TOKEN_ROWS), :] for s in range(TOKEN_ROWS)], axis=1)


def _store_token_rows(ref, row0, val):
    n = val.shape[0]
    for s in range(TOKEN_ROWS):
        ref[pl.ds(row0 + s, n, stride=TOKEN_ROWS), :] = val[:, s * LANES:(s + 1) * LANES]


def _mixer_kernel(x_ref, n1g_ref, w_in_ref, vng_ref, gmat_ref, ws_ref, gbias_ref, gm_wo_ref,
                  dww_ref, dwb_ref, lng_ref, lnb_ref, cf_wo_ref, scw_ref, sc_wo_ref, w_o_ref,
                  n2g_ref, wrt_ref, rbt_ref,
                  x2_ref, route_ref, cnt_ref,
                  aslab, sslab, base, *, tiles_per_seq, rows_in):
    i = pl.program_id(0)
    first = (i % tiles_per_seq) == 0
    nb = TM // GM_BLOCK
    lane_groups = CF_WIDTH // LANES

    class Slab:
        def __init__(self, ref, hist_rows):
            self.ref, self.hist_rows = ref, hist_rows

        def _idx(self, group, row, n):
            start = SLAB_STRIDE * (group * self.hist_rows + row)
            return (pl.ds(start, n, stride=SLAB_STRIDE), slice(None))

        def load(self, group, row, n):
            return self.ref[self._idx(group, row, n)]

        def store(self, group, row, val):
            self.ref[self._idx(group, row, val.shape[0])] = val

    a_hist = Slab(aslab, TM + CF_TAIL)
    s_hist = Slab(sslab, TM + SC_TAIL)

    @pl.when(first)
    def _():
        for g in range(lane_groups):
            a_hist.store(g, 0, jnp.zeros((CF_TAIL, LANES), F32))
            s_hist.store(g, 0, jnp.zeros((SC_TAIL, LANES), F32))

    @pl.when(jnp.logical_not(first))
    def _():
        for g in range(lane_groups):
            a_hist.store(g, 0, a_hist.load(g, TM, CF_TAIL))
            s_hist.store(g, 0, s_hist.load(g, TM, SC_TAIL))

    @pl.when(i == 0)
    def _():
        base[...] = jnp.zeros(base.shape, F32)

    x = _load_token_rows(x_ref, 0, TM) if rows_in else x_ref[...]
    h = x * lax.rsqrt(jnp.mean(x * x, axis=-1, keepdims=True) + EPS) * n1g_ref[...]
    hb = h.astype(BF16)

    def proj(c0, width):
        return _dot(hb, w_in_ref[:, c0:c0 + width])

    glu = proj(C_CA, CF_WIDTH) * jax.nn.sigmoid(proj(C_CG, CF_WIDTH))
    for g in range(lane_groups):
        a_hist.store(g, CF_TAIL, glu[:, g * LANES:(g + 1) * LANES])
    pu, pv = proj(C_GU, GM_WIDTH), proj(C_GV, GM_WIDTH)
    s_b, s_c, s_h = proj(C_SB, SC_WIDTH), proj(C_SC, SC_WIDTH), proj(C_SH, SC_WIDTH)

    ngrp = CONV_ROWS // SUBLANES
    conv_rows = []
    for c in range(TM // CONV_ROWS):
        acc = [[None] * lane_groups for _ in range(ngrp)]
        for k in range(CF_KERNEL):
            shift = k + CF_TAIL - (CF_KERNEL - 1)
            for g in range(lane_groups):
                wk = dww_ref[k * SUBLANES:(k + 1) * SUBLANES, g * LANES:(g + 1) * LANES]
                for r in range(ngrp):
                    val = a_hist.load(g, c * CONV_ROWS + r * SUBLANES + shift, SUBLANES)
                    acc[r][g] = wk * val if acc[r][g] is None else acc[r][g] + wk * val
        conv_rows.extend(jnp.concatenate(a, axis=1) for a in acc)

    sq_hi, sq_lo = _split_bf16(pv * pv)
    ms = _dot(sq_hi, gmat_ref[...]) + _dot(sq_lo, gmat_ref[...])
    cv = jnp.concatenate(conv_rows, axis=0) + dwb_ref[...]
    mu = jnp.mean(cv, axis=-1, keepdims=True)
    cc = cv - mu
    var = jnp.mean(cc * cc, axis=-1, keepdims=True)
    sc_in = s_c * s_h
    conv3 = []
    for g in range(lane_groups):
        s_hist.store(g, SC_TAIL, sc_in[:, g * LANES:(g + 1) * LANES])
        taps = [scw_ref[k:k + 1, g * LANES:(g + 1) * LANES] * s_hist.load(g, SC_TAIL - (SC_KERNEL - 1) + k, TM)
                for k in range(SC_KERNEL)]
        conv3.append((taps[0] + taps[1]) + taps[2])
    y_sc = _dot((s_b * jnp.concatenate(conv3, axis=1)).astype(BF16), sc_wo_ref[...])
    g_gm, g_cf, g_sc = (proj(C_GATES + j * D_MODEL, D_MODEL) for j in range(N_BRANCH))
    ln = cc * lax.rsqrt(var + EPS) * lng_ref[...] + lnb_ref[...]
    y_cf = _dot(jax.nn.silu(ln).astype(BF16), cf_wo_ref[...])

    v = (pv * lax.rsqrt(ms + EPS) * vng_ref[...]).astype(BF16)
    ri = lax.broadcasted_iota(jnp.int32, (GM_BLOCK, GM_BLOCK), 0) // CHUNK
    ci = lax.broadcasted_iota(jnp.int32, (GM_BLOCK, GM_BLOCK), 1) // CHUNK
    causal = ci <= ri
    lane_in_pair = lax.broadcasted_iota(jnp.int32, (GM_BLOCK, nb * LANES), 1) % LANES
    first_head = lane_in_pair < GM_HEAD_DIM
    zparts = []
    for p in range(GM_HEADS // 2):
        w2 = jnp.concatenate(
            [jnp.where(causal, ws_ref[2 * p], 0.0), jnp.where(causal, ws_ref[2 * p + 1], 0.0)],
            axis=0).astype(BF16)
        vcat = jnp.concatenate(
            [v[n * GM_BLOCK:(n + 1) * GM_BLOCK, p * LANES:(p + 1) * LANES] for n in range(nb)], axis=1)
        r = _dot(w2, vcat)
        zparts.append(jnp.where(first_head, r[:GM_BLOCK], r[GM_BLOCK:]))
    z = jnp.concatenate(
        [jnp.concatenate([zp[:, n * LANES:(n + 1) * LANES] for zp in zparts], axis=1) + gbias_ref[...]
         for n in range(nb)], axis=0)
    y_gm = _dot((pu * z).astype(BF16), gm_wo_ref[...])

    m = (jax.nn.sigmoid(g_gm) * y_gm + jax.nn.sigmoid(g_cf) * y_cf) + jax.nn.sigmoid(g_sc) * y_sc
    x2 = x + _dot(m.astype(BF16), w_o_ref[...])
    _store_token_rows(x2_ref, 0, x2)

    h2 = x2 * lax.rsqrt(jnp.mean(x2 * x2, axis=-1, keepdims=True) + EPS) * n2g_ref[...]
    h_hi, h_lo = _split_bf16(h2)
    by_hi = _dot_nt(wrt_ref[...], h_hi)
    by_lo = _dot_nt(wrt_ref[0:ROUTE_ROWS, :], h_lo)
    lt = (by_hi[:ROUTE_ROWS] + by_hi[ROUTE_ROWS:]) + by_lo + rbt_ref[:, 0:1]
    row = [lt[j:j + 1, :] for j in range(N_GROUPS + N_EXPERTS)]

    def first_argmax(vals, skip=None):
        best_v = jnp.full_like(vals[0], -jnp.inf)
        best_i = jnp.zeros_like(vals[0])
        for j, vj in enumerate(vals):
            better = vj > best_v
            if skip is not None:
                better = better & (skip != float(j))
            best_i = jnp.where(better, float(j), best_i)
            best_v = jnp.where(better, vj, best_v)
        return best_i, best_v

    gidx, _ = first_argmax(row[:N_GROUPS])
    in_group = [
        jnp.where(gidx == 0.0, row[N_GROUPS + j],
                  jnp.where(gidx == 1.0, row[N_GROUPS + EXPERTS_PER_GROUP + j],
                            jnp.where(gidx == 2.0, row[N_GROUPS + 2 * EXPERTS_PER_GROUP + j],
                                      row[N_GROUPS + 3 * EXPERTS_PER_GROUP + j])))
        for j in range(EXPERTS_PER_GROUP)]
    i1, _ = first_argmax(in_group)
    i2, _ = first_argmax(in_group, skip=i1)
    p_lo = jnp.minimum(i1, i2)
    p_hi = jnp.maximum(i1, i2)
    pair = p_lo * (7.0 - p_lo) * 0.5 + p_hi - p_lo - 1.0
    bucket = gidx * N_PAIRS + pair

    hit = lax.broadcasted_iota(jnp.int32, (ROUTE_ROWS, TM), 0).astype(F32) == bucket
    onehot = hit.astype(BF16)
    upper = (lax.broadcasted_iota(jnp.int32, (TM, TM), 0)
             <= lax.broadcasted_iota(jnp.int32, (TM, TM), 1)).astype(BF16)
    prefix = _dot(onehot, upper)
    rank = jnp.sum(jnp.where(hit, prefix - 1.0 + base[:, 0:1], 0.0), axis=0, keepdims=True)
    base[...] = base[...] + jnp.sum(hit.astype(F32), axis=1, keepdims=True)
    packed = bucket * float(1 << RANK_BITS) + rank
    route_ref[...] = packed.astype(jnp.int32)
    cnt_ref[...] = base[...].astype(jnp.int32)


def _mixer_call(x, layer, p, tiles_per_seq, rows_in):
    n = p["n_tokens"]
    grid = (n // TM,)

    def const2(shape):
        return pl.BlockSpec(shape, lambda i: (0, 0))

    def layer2(shape):
        return pl.BlockSpec((None,) + shape, lambda i: (layer, 0, 0))

    def layer3(shape):
        return pl.BlockSpec((None,) + shape, lambda i: (layer, 0, 0, 0))

    x_spec = (pl.BlockSpec((TM * TOKEN_ROWS, LANES), lambda i: (i, 0)) if rows_in
              else pl.BlockSpec((TM, D_MODEL), lambda i: (i, 0)))
    in_specs = [
        x_spec,
        layer2((1, D_MODEL)),
        layer2((D_MODEL, W_IN_COLS)),
        layer2((1, GM_WIDTH)),
        const2((GM_WIDTH, GM_WIDTH)),
        layer3((GM_HEADS, GM_BLOCK, GM_BLOCK)),
        layer2((GM_BLOCK, GM_WIDTH)),
        layer2((GM_WIDTH, D_MODEL)),
        layer2((CF_KERNEL * SUBLANES, CF_WIDTH)),
        layer2((1, CF_WIDTH)),
        layer2((1, CF_WIDTH)),
        layer2((1, CF_WIDTH)),
        layer2((CF_WIDTH, D_MODEL)),
        layer2((SC_KERNEL, SC_WIDTH)),
        layer2((SC_WIDTH, D_MODEL)),
        layer2((D_MODEL, D_MODEL)),
        layer2((1, D_MODEL)),
        layer2((2 * ROUTE_ROWS, D_MODEL)),
        layer2((ROUTE_ROWS, LANES)),
    ]
    out_specs = [
        pl.BlockSpec((TM * TOKEN_ROWS, LANES), lambda i: (i, 0)),
        pl.BlockSpec((None, 1, TM), lambda i: (i, 0, 0)),
        pl.BlockSpec((ROUTE_ROWS, LANES), lambda i: (0, 0)),
    ]
    out_shape = [
        jax.ShapeDtypeStruct((n * TOKEN_ROWS, LANES), F32),
        jax.ShapeDtypeStruct((n // TM, 1, TM), jnp.int32),
        jax.ShapeDtypeStruct((ROUTE_ROWS, LANES), jnp.int32),
    ]
    lane_groups = CF_WIDTH // LANES
    scratch = [
        pltpu.VMEM((lane_groups * SLAB_STRIDE * (TM + CF_TAIL), LANES), F32),
        pltpu.VMEM((lane_groups * SLAB_STRIDE * (TM + SC_TAIL), LANES), F32),
        pltpu.VMEM((ROUTE_ROWS, LANES), F32),
    ]
    return pl.pallas_call(
        functools.partial(_mixer_kernel, tiles_per_seq=tiles_per_seq, rows_in=rows_in),
        grid=grid, in_specs=in_specs, out_specs=out_specs, out_shape=out_shape,
        scratch_shapes=scratch,
        compiler_params=pltpu.CompilerParams(
            dimension_semantics=("arbitrary",), vmem_limit_bytes=VMEM_LIMIT),
        name=f"mixer_l{layer}",
    )(x, p["norm1_g"], p["w_in"], p["gm_vnorm_g"], p["gmat"], p["gm_ws"], p["gm_bias"], p["gm_w_out"],
      p["cf_dw_w"], p["cf_dw_b"], p["cf_ln_g"], p["cf_ln_b"], p["cf_w_out"], p["sc_conv_w"],
      p["sc_w_out"], p["w_o"], p["norm2_g"], p["wrt"], p["rbt"])


def _moe_kernel(tinfo_ref, nt_ref, tbase_ref,
                packed_ref, x2_hbm, n2g_ref, wr_cat_ref, rb_ref,
                wga_ref, wua_ref, wda_ref, wgb_ref, wub_ref, wdb_ref, fg_ref,
                out_hbm,
                xbuf, obuf, pos_v, pos_s, src_s, gsem, ssem, psem, *, final, n_tokens):
    t = pl.program_id(0)
    nt = nt_ref[0]
    slot = t % 2

    def valid_rows(tile):
        return tinfo_ref[4 * tile + 3]

    def token_rows(hbm, slot0, u):
        tok = src_s[slot0 + u]
        return hbm.at[pl.ds(pl.multiple_of(tok * TOKEN_ROWS, TOKEN_ROWS), TOKEN_ROWS), :]

    def buf_rows(buf, row0, u):
        return buf.at[pl.ds(pl.multiple_of((row0 + u) * TOKEN_ROWS, TOKEN_ROWS), TOKEN_ROWS), :]

    def gather_row(slot0, row0, u, sl):
        return pltpu.make_async_copy(token_rows(x2_hbm, slot0, u), buf_rows(xbuf, row0, u), gsem.at[sl])

    def scatter_row(slot0, row0, u, sl):
        return pltpu.make_async_copy(buf_rows(obuf, row0, u), token_rows(out_hbm, slot0, u), ssem.at[sl])

    def for_rows(make, tile, sl, act):
        count = valid_rows(tile)
        groups = lax.shift_right_logical(count, DMA_UNROLL.bit_length() - 1)
        slot0, row0 = tile * TR, sl * TR

        def group(g, c):
            for u in range(DMA_UNROLL):
                act(make(slot0 + g * DMA_UNROLL, row0 + g * DMA_UNROLL, u, sl))
            return c
        lax.fori_loop(0, groups, group, 0)

        def single(r, c):
            act(make(slot0, row0, r, sl))
            return c
        lax.fori_loop(groups * DMA_UNROLL, count, single, 0)

    def start_rows(make, tile, sl):
        for_rows(make, tile, sl, lambda cp: cp.start())

    def wait_rows(make, tile, sl):
        for_rows(make, tile, sl, lambda cp: cp.wait())

    @pl.when(t == 0)
    def _():
        word = packed_ref[...]
        bkt = lax.shift_right_logical(word, RANK_BITS)
        pos = word & ((1 << RANK_BITS) - 1)
        for b in range(N_BUCKETS):
            pos = pos + jnp.where(bkt == b, tbase_ref[b], 0)
        pos_v[...] = pos
        to_smem = pltpu.make_async_copy(pos_v, pos_s, psem)
        to_smem.start()
        to_smem.wait()

        def fill(r, c):
            for u in range(LANES):
                src_s[pos_s[r, u]] = r * LANES + u
            return c
        lax.fori_loop(0, n_tokens // LANES, fill, 0)
        xbuf[...] = jnp.zeros(xbuf.shape, F32)

    @pl.when(t < nt)
    def _():
        @pl.when(t == 0)
        def _():
            start_rows(gather_row, t, slot)

        @pl.when(t + 1 < nt)
        def _():
            start_rows(gather_row, t + 1, 1 - slot)

        wait_rows(gather_row, t, slot)

        @pl.when(t >= 2)
        def _():
            wait_rows(scatter_row, t - 2, slot)

        grp = tinfo_ref[4 * t]
        ea = tinfo_ref[4 * t + 1]
        eb = tinfo_ref[4 * t + 2]
        xr = _load_token_rows(xbuf, slot * (TR * TOKEN_ROWS), TR)
        h2, logits = _router_logits(xr, n2g_ref[...], wr_cat_ref[...], rb_ref[...])
        lane = lax.broadcasted_iota(jnp.int32, (TR, LANES), 1)
        is_g = lane < N_GROUPS
        gmax = jnp.max(jnp.where(is_g, logits, -jnp.inf), axis=-1, keepdims=True)
        eg = jnp.where(is_g, jnp.exp(logits - gmax), 0.0)
        p_top = (jnp.sum(jnp.where(lane == grp, eg, 0.0), axis=-1, keepdims=True)
                 / jnp.sum(eg, axis=-1, keepdims=True))
        la = jnp.sum(jnp.where(lane == N_GROUPS + ea, logits, 0.0), axis=-1, keepdims=True)
        lb = jnp.sum(jnp.where(lane == N_GROUPS + eb, logits, 0.0), axis=-1, keepdims=True)
        mx = jnp.maximum(la, lb)
        pa = jnp.exp(la - mx)
        pb = jnp.exp(lb - mx)
        wa = pa / (pa + pb) * p_top
        wb = pb / (pa + pb) * p_top

        hb = h2.astype(BF16)
        def w(ref):
            return ref[...].astype(BF16)

        act_a = jax.nn.silu(_dot(hb, w(wga_ref))) * _dot(hb, w(wua_ref))
        act_b = jax.nn.silu(_dot(hb, w(wgb_ref))) * _dot(hb, w(wub_ref))
        y = _dot((wa * act_a).astype(BF16), w(wda_ref)) + _dot((wb * act_b).astype(BF16), w(wdb_ref))
        res = xr + y
        if final:
            res = res * lax.rsqrt(jnp.mean(res * res, axis=-1, keepdims=True) + EPS) * fg_ref[...]
        _store_token_rows(obuf, slot * (TR * TOKEN_ROWS), res)

        start_rows(scatter_row, t, slot)

        @pl.when(t == nt - 1)
        def _():
            @pl.when(t >= 1)
            def _():
                wait_rows(scatter_row, t - 1, 1 - slot)
            wait_rows(scatter_row, t, slot)


def _moe_call(x2, layer, p, tinfo, nt, tbase, packed, final):
    n = p["n_tokens"]
    max_tiles = n // TR + N_BUCKETS

    def const2(shape):
        return pl.BlockSpec(shape, lambda t, *_: (0, 0))

    def layer2(shape):
        return pl.BlockSpec((None,) + shape, lambda t, *_: (layer, 0, 0))

    def expert(shape, which):
        def imap(t, tinfo_ref, nt_ref, tbase_ref):
            tt = jnp.minimum(t, nt_ref[0] - 1)
            return (layer, tinfo_ref[4 * tt + which], 0, 0)
        return pl.BlockSpec((None, None) + shape, imap)

    in_specs = [
        const2((n // LANES, LANES)),
        pl.BlockSpec(memory_space=pl.ANY),
        layer2((1, D_MODEL)),
        layer2((D_MODEL, 2 * LANES)),
        layer2((1, LANES)),
        expert((D_MODEL, D_EXPERT), 1),
        expert((D_MODEL, D_EXPERT), 1),
        expert((D_EXPERT, D_MODEL), 1),
        expert((D_MODEL, D_EXPERT), 2),
        expert((D_MODEL, D_EXPERT), 2),
        expert((D_EXPERT, D_MODEL), 2),
        const2((1, D_MODEL)),
    ]
    grid_spec = pltpu.PrefetchScalarGridSpec(
        num_scalar_prefetch=3, grid=(max_tiles,), in_specs=in_specs,
        out_specs=pl.BlockSpec(memory_space=pl.ANY),
        scratch_shapes=[
            pltpu.VMEM((2 * TR * TOKEN_ROWS, LANES), F32),
            pltpu.VMEM((2 * TR * TOKEN_ROWS, LANES), F32),
            pltpu.VMEM((n // LANES, LANES), jnp.int32),
            pltpu.SMEM((n // LANES, LANES), jnp.int32),
            pltpu.SMEM((max_tiles * TR,), jnp.int32),
            pltpu.SemaphoreType.DMA((2,)),
            pltpu.SemaphoreType.DMA((2,)),
            pltpu.SemaphoreType.DMA(()),
        ])
    return pl.pallas_call(
        functools.partial(_moe_kernel, final=final, n_tokens=n),
        grid_spec=grid_spec,
        out_shape=jax.ShapeDtypeStruct((n * TOKEN_ROWS, LANES), F32),
        compiler_params=pltpu.CompilerParams(
            dimension_semantics=("arbitrary",), vmem_limit_bytes=VMEM_LIMIT),
        name=f"moe_l{layer}",
    )(tinfo, nt, tbase, packed, x2, p["norm2_g"], p["wr_cat"], p["rb"],
      p["exp_w_gate"], p["exp_w_up"], p["exp_w_down"], p["exp_w_gate"], p["exp_w_up"], p["exp_w_down"],
      p["final_norm_g"])


def _tile_tables(counts, n):
    max_tiles = n // TR + N_BUCKETS
    order = np.asarray([g * N_PAIRS + q for g in range(N_GROUPS) for q in PAIR_ORDER], np.int32)
    cnt = counts[:N_BUCKETS, 0][order]
    tiles = (cnt + TR - 1) // TR
    tile_end = jnp.cumsum(tiles)
    tile_start = tile_end - tiles
    nt = tile_end[-1:].astype(jnp.int32)
    tile_ids = jnp.arange(max_tiles, dtype=jnp.int32)
    tpos = jnp.sum((tile_ids[:, None] >= tile_end[None, :]).astype(jnp.int32), axis=1)
    tpos = jnp.minimum(tpos, N_BUCKETS - 1)
    grp = jnp.asarray(order // N_PAIRS)[tpos]
    pr = jnp.asarray(order % N_PAIRS)[tpos]
    ea = grp * EXPERTS_PER_GROUP + jnp.asarray(PAIR_A, jnp.int32)[pr]
    eb = grp * EXPERTS_PER_GROUP + jnp.asarray(PAIR_B, jnp.int32)[pr]
    valid = jnp.clip(cnt[tpos] - (tile_ids - tile_start[tpos]) * TR, 0, TR)
    tinfo = jnp.stack([grp, ea, eb, valid], axis=1).reshape(-1).astype(jnp.int32)
    tbase = (tile_start.astype(jnp.int32) * TR)[np.argsort(order)]
    return tinfo, nt, tbase


def kernel(x, norm1_g, w_in, gm_vnorm_g, gm_ws, gm_bs, gm_w_out, cf_dw_w, cf_dw_b, cf_ln_g, cf_ln_b,
           cf_w_out, sc_conv_w, sc_w_out, w_o, norm2_g, router_g, router_g_b, router_e, router_e_b,
           exp_w_gate, exp_w_up, exp_w_down, final_norm_g):
    b, s, d = x.shape
    depth = w_in.shape[0]
    n = b * s
    assert d == D_MODEL and s % TM == 0 and TM % GM_BLOCK == 0 and n % TR == 0
    assert n <= (1 << RANK_BITS) and TR % DMA_UNROLL == 0 and n % LANES == 0

    wr = jnp.concatenate([router_g, router_e], axis=-1)
    wr = jnp.pad(wr, ((0, 0), (0, 0), (0, LANES - wr.shape[-1])))
    wr_hi = wr.astype(BF16)
    wr_lo = (wr - wr_hi.astype(F32)).astype(BF16)
    rb = jnp.concatenate([router_g_b, router_e_b], axis=-1)
    rbt = jnp.broadcast_to(jnp.pad(rb, ((0, 0), (0, ROUTE_ROWS - rb.shape[-1])))[:, :, None],
                           (depth, ROUTE_ROWS, LANES))
    rb = jnp.pad(rb, ((0, 0), (0, LANES - rb.shape[-1])))[:, None, :]
    wrt = jnp.swapaxes(jnp.concatenate([wr_hi[:, :, :ROUTE_ROWS], wr_lo[:, :, :ROUTE_ROWS]], axis=-1), 1, 2)
    head = np.arange(GM_WIDTH) // GM_HEAD_DIM
    gmat = jnp.asarray((head[:, None] == head[None, :]).astype(np.float32) / GM_HEAD_DIM, BF16)

    p = dict(
        n_tokens=n,
        norm1_g=norm1_g[:, None, :],
        w_in=w_in.astype(BF16),
        gm_vnorm_g=gm_vnorm_g[:, None, :],
        gmat=gmat,
        gm_ws=gm_ws,
        gm_bias=jnp.repeat(jnp.swapaxes(gm_bs, 1, 2), GM_HEAD_DIM, axis=2),
        gm_w_out=gm_w_out.astype(BF16),
        cf_dw_w=jnp.repeat(cf_dw_w, SUBLANES, axis=1),
        cf_dw_b=cf_dw_b[:, None, :],
        cf_ln_g=cf_ln_g[:, None, :],
        cf_ln_b=cf_ln_b[:, None, :],
        cf_w_out=cf_w_out.astype(BF16),
        sc_conv_w=sc_conv_w,
        sc_w_out=sc_w_out.astype(BF16),
        w_o=w_o.astype(BF16),
        norm2_g=norm2_g[:, None, :],
        wrt=wrt, rbt=rbt,
        wr_cat=jnp.concatenate([wr_hi, wr_lo], axis=-1), rb=rb,
        exp_w_gate=exp_w_gate,
        exp_w_up=exp_w_up,
        exp_w_down=exp_w_down,
        final_norm_g=final_norm_g[None, :],
    )

    xf = x.reshape(n, d)
    for layer in range(depth):
        x2, route, counts = _mixer_call(xf, layer, p, s // TM, rows_in=(layer > 0))
        tinfo, nt, tbase = _tile_tables(counts, n)
        xf = _moe_call(x2, layer, p, tinfo, nt, tbase, route.reshape(n // LANES, LANES),
                       final=(layer == depth - 1))
    return xf.reshape(b, s, d)
```

```python
import functools

import numpy as np
import jax
import jax.numpy as jnp
from jax import lax
from jax.experimental import pallas as pl
from jax.experimental.pallas import tpu as pltpu

F32 = jnp.float32
BF16 = jnp.bfloat16

D_MODEL = 1024
CHUNK = 64
GM_BLOCK = 128
GM_HEADS = 8
GM_WIDTH = 512
GM_HEAD_DIM = GM_WIDTH // GM_HEADS
CF_WIDTH = 512
CF_KERNEL = 31
SC_WIDTH = 512
SC_KERNEL = 3
N_BRANCH = 3
N_GROUPS = 4
EXPERTS_PER_GROUP = 4
N_EXPERTS = N_GROUPS * EXPERTS_PER_GROUP
D_EXPERT = 512
EPS = 1e-6

C_GU, C_GV, C_CA, C_CG, C_SB, C_SC, C_SH, C_GATES = 0, 512, 1024, 1536, 2048, 2560, 3072, 3584
W_IN_COLS = C_GATES + N_BRANCH * D_MODEL

LANES = 128
SUBLANES = 8
TOKEN_ROWS = D_MODEL // LANES
N_PAIRS = 6
N_BUCKETS = N_GROUPS * N_PAIRS
PAIR_A = (0, 0, 0, 1, 1, 3)
PAIR_B = (1, 2, 3, 2, 3, 2)
PAIR_ORDER = (0, 1, 2, 4, 3, 5)
RANK_BITS = 16
ROUTE_ROWS = 32

TM = 256
TR = 256
CF_TAIL = 32
SC_TAIL = 8
CONV_ROWS = 32
W_CHUNK = 512
SLAB_STRIDE = 2
DMA_UNROLL = 16
VMEM_LIMIT = 56 * 1024 * 1024


def _dot(a, b):
    return jnp.dot(a, b, preferred_element_type=F32)


def _dot_nt(a, b):
    return lax.dot_general(a, b, (((1,), (1,)), ((), ())), preferred_element_type=F32)


def _split_bf16(v):
    hi = v.astype(BF16)
    lo = (v - hi.astype(F32)).astype(BF16)
    return hi, lo


def _router_logits(x2, n2g, wr_cat, rb):
    h2 = x2 * lax.rsqrt(jnp.mean(x2 * x2, axis=-1, keepdims=True) + EPS) * n2g
    hi, lo = _split_bf16(h2)
    both = _dot(hi, wr_cat)
    logits = both[:, :LANES] + both[:, LANES:] + _dot(lo, wr_cat[:, :LANES]) + rb
    return h2, logits


def _load_token_rows(ref, row0, n):
    return jnp.concatenate(
        [ref[pl.ds(row0 + s, n, stride=TOKEN_ROWS), :] for s in range(TOKEN_ROWS)], axis=1)


def _store_token_rows(ref, row0, val):
    n = val.shape[0]
    for s in range(TOKEN_ROWS):
        ref[pl.ds(row0 + s, n, stride=TOKEN_ROWS), :] = val[:, s * LANES:(s + 1) * LANES]


def _mixer_kernel(x_ref, n1g_ref, w_in_hbm, vng_ref, gmat_ref, ws_ref, gbias_ref, gm_wo_f32,
                  dww_ref, dwb_ref, lng_ref, lnb_ref, cf_wo_f32, scw_ref, sc_wo_f32, w_o_f32,
                  n2g_ref, wrt_ref, rbt_ref,
                  x2_ref, route_ref, cnt_ref,
                  aslab, sslab, base, w_in_ref, gm_wo_ref, cf_wo_ref, sc_wo_ref, w_o_ref, stage, wsem,
                  *, layer, tiles_per_seq, rows_in):
    i = pl.program_id(0)
    first = (i % tiles_per_seq) == 0
    nb = TM // GM_BLOCK
    lane_groups = CF_WIDTH // LANES

    @pl.when(i == 0)
    def _():
        n_chunks = W_IN_COLS // W_CHUNK

        def chunk_copy(c):
            return pltpu.make_async_copy(
                w_in_hbm.at[layer, :, pl.ds(c * W_CHUNK, W_CHUNK)], stage.at[c % 2], wsem.at[c % 2])

        chunk_copy(0).start()
        for c in range(n_chunks):
            if c + 1 < n_chunks:
                chunk_copy(c + 1).start()
            chunk_copy(c).wait()
            w_in_ref[:, c * W_CHUNK:(c + 1) * W_CHUNK] = stage[c % 2].astype(BF16)
        gm_wo_ref[...] = gm_wo_f32[...].astype(BF16)
        cf_wo_ref[...] = cf_wo_f32[...].astype(BF16)
        sc_wo_ref[...] = sc_wo_f32[...].astype(BF16)
        w_o_ref[...] = w_o_f32[...].astype(BF16)

    class Slab:
        def __init__(self, ref, hist_rows):
            self.ref, self.hist_rows = ref, hist_rows

        def _idx(self, group, row, n):
            start = SLAB_STRIDE * (group * self.hist_rows + row)
            return (pl.ds(start, n, stride=SLAB_STRIDE), slice(None))

        def load(self, group, row, n):
            return self.ref[self._idx(group, row, n)]

        def store(self, group, row, val):
            self.ref[self._idx(group, row, val.shape[0])] = val

    a_hist = Slab(aslab, TM + CF_TAIL)
    s_hist = Slab(sslab, TM + SC_TAIL)

    @pl.when(first)
    def _():
        for g in range(lane_groups):
            a_hist.store(g, 0, jnp.zeros((CF_TAIL, LANES), F32))
            s_hist.store(g, 0, jnp.zeros((SC_TAIL, LANES), F32))

    @pl.when(jnp.logical_not(first))
    def _():
        for g in range(lane_groups):
            a_hist.store(g, 0, a_hist.load(g, TM, CF_TAIL))
            s_hist.store(g, 0, s_hist.load(g, TM, SC_TAIL))

    @pl.when(i == 0)
    def _():
        base[...] = jnp.zeros(base.shape, F32)

    x = _load_token_rows(x_ref, 0, TM) if rows_in else x_ref[...]
    h = x * lax.rsqrt(jnp.mean(x * x, axis=-1, keepdims=True) + EPS) * n1g_ref[...]
    hb = h.astype(BF16)

    def proj(c0, width):
        return _dot(hb, w_in_ref[:, c0:c0 + width])

    glu = proj(C_CA, CF_WIDTH) * jax.nn.sigmoid(proj(C_CG, CF_WIDTH))
    for g in range(lane_groups):
        a_hist.store(g, CF_TAIL, glu[:, g * LANES:(g + 1) * LANES])
    pu, pv = proj(C_GU, GM_WIDTH), proj(C_GV, GM_WIDTH)
    s_b, s_c, s_h = proj(C_SB, SC_WIDTH), proj(C_SC, SC_WIDTH), proj(C_SH, SC_WIDTH)

    ngrp = CONV_ROWS // SUBLANES
    conv_rows = []
    for c in range(TM // CONV_ROWS):
        acc = [[None] * lane_groups for _ in range(ngrp)]
        for k in range(CF_KERNEL):
            shift = k + CF_TAIL - (CF_KERNEL - 1)
            for g in range(lane_groups):
                wk = dww_ref[k * SUBLANES:(k + 1) * SUBLANES, g * LANES:(g + 1) * LANES]
                for r in range(ngrp):
                    val = a_hist.load(g, c * CONV_ROWS + r * SUBLANES + shift, SUBLANES)
                    acc[r][g] = wk * val if acc[r][g] is None else acc[r][g] + wk * val
        conv_rows.extend(jnp.concatenate(a, axis=1) for a in acc)

    sq_hi, sq_lo = _split_bf16(pv * pv)
    ms = _dot(sq_hi, gmat_ref[...]) + _dot(sq_lo, gmat_ref[...])
    cv = jnp.concatenate(conv_rows, axis=0) + dwb_ref[...]
    mu = jnp.mean(cv, axis=-1, keepdims=True)
    cc = cv - mu
    var = jnp.mean(cc * cc, axis=-1, keepdims=True)
    sc_in = s_c * s_h
    conv3 = []
    for g in range(lane_groups):
        s_hist.store(g, SC_TAIL, sc_in[:, g * LANES:(g + 1) * LANES])
        taps = [scw_ref[k:k + 1, g * LANES:(g + 1) * LANES] * s_hist.load(g, SC_TAIL - (SC_KERNEL - 1) + k, TM)
                for k in range(SC_KERNEL)]
        conv3.append((taps[0] + taps[1]) + taps[2])
    y_sc = _dot((s_b * jnp.concatenate(conv3, axis=1)).astype(BF16), sc_wo_ref[...])
    def route_tile(xp):
        h2 = xp * lax.rsqrt(jnp.mean(xp * xp, axis=-1, keepdims=True) + EPS) * n2g_ref[...]
        h_hi, h_lo = _split_bf16(h2)
        by_hi = _dot_nt(wrt_ref[...], h_hi)
        by_lo = _dot_nt(wrt_ref[0:ROUTE_ROWS, :], h_lo)
        lt = (by_hi[:ROUTE_ROWS] + by_hi[ROUTE_ROWS:]) + by_lo + rbt_ref[:, 0:1]
        row = [lt[j:j + 1, :] for j in range(N_GROUPS + N_EXPERTS)]

        def first_argmax(vals, skip=None):
            best_v = jnp.full_like(vals[0], -jnp.inf)
            best_i = jnp.zeros_like(vals[0])
            for j, vj in enumerate(vals):
                better = vj > best_v
                if skip is not None:
                    better = better & (skip != float(j))
                best_i = jnp.where(better, float(j), best_i)
                best_v = jnp.where(better, vj, best_v)
            return best_i, best_v

        gidx, _ = first_argmax(row[:N_GROUPS])
        in_group = [
            jnp.where(gidx == 0.0, row[N_GROUPS + j],
                      jnp.where(gidx == 1.0, row[N_GROUPS + EXPERTS_PER_GROUP + j],
                                jnp.where(gidx == 2.0, row[N_GROUPS + 2 * EXPERTS_PER_GROUP + j],
                                          row[N_GROUPS + 3 * EXPERTS_PER_GROUP + j])))
            for j in range(EXPERTS_PER_GROUP)]
        i1, _ = first_argmax(in_group)
        i2, _ = first_argmax(in_group, skip=i1)
        p_lo = jnp.minimum(i1, i2)
        p_hi = jnp.maximum(i1, i2)
        pair = p_lo * (7.0 - p_lo) * 0.5 + p_hi - p_lo - 1.0
        bucket = gidx * N_PAIRS + pair

        hit = lax.broadcasted_iota(jnp.int32, (ROUTE_ROWS, TM), 0).astype(F32) == bucket
        onehot = hit.astype(BF16)
        upper = (lax.broadcasted_iota(jnp.int32, (TM, TM), 0)
                 <= lax.broadcasted_iota(jnp.int32, (TM, TM), 1)).astype(BF16)
        prefix = _dot(onehot, upper)
        rank = jnp.sum(jnp.where(hit, prefix - 1.0 + base[:, 0:1], 0.0), axis=0, keepdims=True)
        base[...] = base[...] + jnp.sum(hit.astype(F32), axis=1, keepdims=True)
        packed = bucket * float(1 << RANK_BITS) + rank
        route_ref[...] = packed.astype(jnp.int32)
        cnt_ref[...] = base[...].astype(jnp.int32)

    g_gm, g_cf, g_sc = (proj(C_GATES + j * D_MODEL, D_MODEL) for j in range(N_BRANCH))
    ln = cc * lax.rsqrt(var + EPS) * lng_ref[...] + lnb_ref[...]
    y_cf = _dot(jax.nn.silu(ln).astype(BF16), cf_wo_ref[...])

    v = (pv * lax.rsqrt(ms + EPS) * vng_ref[...]).astype(BF16)
    ri = lax.broadcasted_iota(jnp.int32, (GM_BLOCK, GM_BLOCK), 0) // CHUNK
    ci = lax.broadcasted_iota(jnp.int32, (GM_BLOCK, GM_BLOCK), 1) // CHUNK
    causal = ci <= ri
    lane_in_pair = lax.broadcasted_iota(jnp.int32, (GM_BLOCK, nb * LANES), 1) % LANES
    first_head = lane_in_pair < GM_HEAD_DIM
    zparts = []
    for p in range(GM_HEADS // 2):
        w2 = jnp.concatenate(
            [jnp.where(causal, ws_ref[2 * p], 0.0), jnp.where(causal, ws_ref[2 * p + 1], 0.0)],
            axis=0).astype(BF16)
        vcat = jnp.concatenate(
            [v[n * GM_BLOCK:(n + 1) * GM_BLOCK, p * LANES:(p + 1) * LANES] for n in range(nb)], axis=1)
        r = _dot(w2, vcat)
        zparts.append(jnp.where(first_head, r[:GM_BLOCK], r[GM_BLOCK:]))
    z = jnp.concatenate(
        [jnp.concatenate([zp[:, n * LANES:(n + 1) * LANES] for zp in zparts], axis=1) + gbias_ref[...]
         for n in range(nb)], axis=0)
    y_gm = _dot((pu * z).astype(BF16), gm_wo_ref[...])

    m = (jax.nn.sigmoid(g_gm) * y_gm + jax.nn.sigmoid(g_cf) * y_cf) + jax.nn.sigmoid(g_sc) * y_sc
    x2 = x + _dot(m.astype(BF16), w_o_ref[...])
    _store_token_rows(x2_ref, 0, x2)
    route_tile(x2)


def _mixer_call(x, layer, p, tiles_per_seq, rows_in):
    n = p["n_tokens"]
    grid = (n // TM,)

    def const2(shape):
        return pl.BlockSpec(shape, lambda i: (0, 0))

    def layer2(shape):
        return pl.BlockSpec((None,) + shape, lambda i: (layer, 0, 0))

    def layer3(shape):
        return pl.BlockSpec((None,) + shape, lambda i: (layer, 0, 0, 0))

    x_spec = (pl.BlockSpec((TM * TOKEN_ROWS, LANES), lambda i: (i, 0)) if rows_in
              else pl.BlockSpec((TM, D_MODEL), lambda i: (i, 0)))
    in_specs = [
        x_spec,
        layer2((1, D_MODEL)),
        pl.BlockSpec(memory_space=pl.ANY),
        layer2((1, GM_WIDTH)),
        const2((GM_WIDTH, GM_WIDTH)),
        layer3((GM_HEADS, GM_BLOCK, GM_BLOCK)),
        layer2((GM_BLOCK, GM_WIDTH)),
        layer2((GM_WIDTH, D_MODEL)),
        layer2((CF_KERNEL * SUBLANES, CF_WIDTH)),
        layer2((1, CF_WIDTH)),
        layer2((1, CF_WIDTH)),
        layer2((1, CF_WIDTH)),
        layer2((CF_WIDTH, D_MODEL)),
        layer2((SC_KERNEL, SC_WIDTH)),
        layer2((SC_WIDTH, D_MODEL)),
        layer2((D_MODEL, D_MODEL)),
        layer2((1, D_MODEL)),
        layer2((2 * ROUTE_ROWS, D_MODEL)),
        layer2((ROUTE_ROWS, LANES)),
    ]
    out_specs = [
        pl.BlockSpec((TM * TOKEN_ROWS, LANES), lambda i: (i, 0)),
        pl.BlockSpec((None, 1, TM), lambda i: (i, 0, 0)),
        pl.BlockSpec((ROUTE_ROWS, LANES), lambda i: (0, 0)),
    ]
    out_shape = [
        jax.ShapeDtypeStruct((n * TOKEN_ROWS, LANES), F32),
        jax.ShapeDtypeStruct((n // TM, 1, TM), jnp.int32),
        jax.ShapeDtypeStruct((ROUTE_ROWS, LANES), jnp.int32),
    ]
    lane_groups = CF_WIDTH // LANES
    scratch = [
        pltpu.VMEM((lane_groups * SLAB_STRIDE * (TM + CF_TAIL), LANES), F32),
        pltpu.VMEM((lane_groups * SLAB_STRIDE * (TM + SC_TAIL), LANES), F32),
        pltpu.VMEM((ROUTE_ROWS, LANES), F32),
        pltpu.VMEM((D_MODEL, W_IN_COLS), BF16),
        pltpu.VMEM((GM_WIDTH, D_MODEL), BF16),
        pltpu.VMEM((CF_WIDTH, D_MODEL), BF16),
        pltpu.VMEM((SC_WIDTH, D_MODEL), BF16),
        pltpu.VMEM((D_MODEL, D_MODEL), BF16),
        pltpu.VMEM((2, D_MODEL, W_CHUNK), F32),
        pltpu.SemaphoreType.DMA((2,)),
    ]
    return pl.pallas_call(
        functools.partial(_mixer_kernel, layer=layer, tiles_per_seq=tiles_per_seq, rows_in=rows_in),
        grid=grid, in_specs=in_specs, out_specs=out_specs, out_shape=out_shape,
        scratch_shapes=scratch,
        compiler_params=pltpu.CompilerParams(
            dimension_semantics=("arbitrary",), vmem_limit_bytes=VMEM_LIMIT),
        name=f"mixer_l{layer}",
    )(x, p["norm1_g"], p["w_in"], p["gm_vnorm_g"], p["gmat"], p["gm_ws"], p["gm_bias"], p["gm_w_out"],
      p["cf_dw_w"], p["cf_dw_b"], p["cf_ln_g"], p["cf_ln_b"], p["cf_w_out"], p["sc_conv_w"],
      p["sc_w_out"], p["w_o"], p["norm2_g"], p["wrt"], p["rbt"])


def _moe_kernel(tinfo_ref, nt_ref, tbase_ref,
                packed_ref, x2_hbm, n2g_ref, wr_cat_ref, rb_ref,
                wga_ref, wua_ref, wda_ref, wgb_ref, wub_ref, wdb_ref, fg_ref,
                out_hbm,
                xbuf, obuf, pos_v, pos_s, src_s, gsem, ssem, psem, *, final, n_tokens):
    t = pl.program_id(0)
    nt = nt_ref[0]
    slot = t % 2

    def valid_rows(tile):
        return tinfo_ref[4 * tile + 3]

    def token_rows(hbm, slot0, u):
        tok = src_s[slot0 + u]
        return hbm.at[pl.ds(pl.multiple_of(tok * TOKEN_ROWS, TOKEN_ROWS), TOKEN_ROWS), :]

    def buf_rows(buf, row0, u):
        return buf.at[pl.ds(pl.multiple_of((row0 + u) * TOKEN_ROWS, TOKEN_ROWS), TOKEN_ROWS), :]

    def gather_row(slot0, row0, u, sl):
        return pltpu.make_async_copy(token_rows(x2_hbm, slot0, u), buf_rows(xbuf, row0, u), gsem.at[sl])

    def scatter_row(slot0, row0, u, sl):
        return pltpu.make_async_copy(buf_rows(obuf, row0, u), token_rows(out_hbm, slot0, u), ssem.at[sl])

    def for_rows(make, tile, sl, act):
        count = valid_rows(tile)
        groups = lax.shift_right_logical(count, DMA_UNROLL.bit_length() - 1)
        slot0, row0 = tile * TR, sl * TR

        def group(g, c):
            for u in range(DMA_UNROLL):
                act(make(slot0 + g * DMA_UNROLL, row0 + g * DMA_UNROLL, u, sl))
            return c
        lax.fori_loop(0, groups, group, 0)

        def single(r, c):
            act(make(slot0, row0, r, sl))
            return c
        lax.fori_loop(groups * DMA_UNROLL, count, single, 0)

    def start_rows(make, tile, sl):
        for_rows(make, tile, sl, lambda cp: cp.start())

    def wait_rows(make, tile, sl):
        for_rows(make, tile, sl, lambda cp: cp.wait())

    @pl.when(t == 0)
    def _():
        word = packed_ref[...]
        bkt = lax.shift_right_logical(word, RANK_BITS)
        pos = word & ((1 << RANK_BITS) - 1)
        for b in range(N_BUCKETS):
            pos = pos + jnp.where(bkt == b, tbase_ref[b], 0)
        pos_v[...] = pos
        to_smem = pltpu.make_async_copy(pos_v, pos_s, psem)
        to_smem.start()
        to_smem.wait()

        def fill(r, c):
            for u in range(LANES):
                src_s[pos_s[r, u]] = r * LANES + u
            return c
        lax.fori_loop(0, n_tokens // LANES, fill, 0)
        xbuf[...] = jnp.zeros(xbuf.shape, F32)

    @pl.when(t < nt)
    def _():
        @pl.when(t == 0)
        def _():
            start_rows(gather_row, t, slot)

        @pl.when(t + 1 < nt)
        def _():
            start_rows(gather_row, t + 1, 1 - slot)

        wait_rows(gather_row, t, slot)

        @pl.when(t >= 2)
        def _():
            wait_rows(scatter_row, t - 2, slot)

        grp = tinfo_ref[4 * t]
        ea = tinfo_ref[4 * t + 1]
        eb = tinfo_ref[4 * t + 2]
        def experts(rows):
            xr = _load_token_rows(xbuf, slot * (TR * TOKEN_ROWS), rows)
            h2, logits = _router_logits(xr, n2g_ref[...], wr_cat_ref[...], rb_ref[...])
            lane = lax.broadcasted_iota(jnp.int32, (rows, LANES), 1)
            is_g = lane < N_GROUPS
            gmax = jnp.max(jnp.where(is_g, logits, -jnp.inf), axis=-1, keepdims=True)
            eg = jnp.where(is_g, jnp.exp(logits - gmax), 0.0)
            p_top = (jnp.sum(jnp.where(lane == grp, eg, 0.0), axis=-1, keepdims=True)
                     / jnp.sum(eg, axis=-1, keepdims=True))
            la = jnp.sum(jnp.where(lane == N_GROUPS + ea, logits, 0.0), axis=-1, keepdims=True)
            lb = jnp.sum(jnp.where(lane == N_GROUPS + eb, logits, 0.0), axis=-1, keepdims=True)
            mx = jnp.maximum(la, lb)
            pa = jnp.exp(la - mx)
            pb = jnp.exp(lb - mx)
            wa = pa / (pa + pb) * p_top
            wb = pb / (pa + pb) * p_top

            hb = h2.astype(BF16)
            def w(ref):
                return ref[...].astype(BF16)

            act_a = jax.nn.silu(_dot(hb, w(wga_ref))) * _dot(hb, w(wua_ref))
            act_b = jax.nn.silu(_dot(hb, w(wgb_ref))) * _dot(hb, w(wub_ref))
            y = _dot((wa * act_a).astype(BF16), w(wda_ref)) + _dot((wb * act_b).astype(BF16), w(wdb_ref))
            res = xr + y
            if final:
                res = res * lax.rsqrt(jnp.mean(res * res, axis=-1, keepdims=True) + EPS) * fg_ref[...]
            _store_token_rows(obuf, slot * (TR * TOKEN_ROWS), res)

        @pl.when(valid_rows(t) > TR // 2)
        def _():
            experts(TR)

        @pl.when(valid_rows(t) <= TR // 2)
        def _():
            experts(TR // 2)

        start_rows(scatter_row, t, slot)

        @pl.when(t == nt - 1)
        def _():
            @pl.when(t >= 1)
            def _():
                wait_rows(scatter_row, t - 1, 1 - slot)
            wait_rows(scatter_row, t, slot)


def _moe_call(x2, layer, p, tinfo, nt, tbase, packed, final):
    n = p["n_tokens"]
    max_tiles = n // TR + N_BUCKETS

    def const2(shape):
        return pl.BlockSpec(shape, lambda t, *_: (0, 0))

    def layer2(shape):
        return pl.BlockSpec((None,) + shape, lambda t, *_: (layer, 0, 0))

    def expert(shape, which):
        def imap(t, tinfo_ref, nt_ref, tbase_ref):
            tt = jnp.minimum(t, nt_ref[0] - 1)
            return (layer, tinfo_ref[4 * tt + which], 0, 0)
        return pl.BlockSpec((None, None) + shape, imap)

    in_specs = [
        const2((n // LANES, LANES)),
        pl.BlockSpec(memory_space=pl.ANY),
        layer2((1, D_MODEL)),
        layer2((D_MODEL, 2 * LANES)),
        layer2((1, LANES)),
        expert((D_MODEL, D_EXPERT), 1),
        expert((D_MODEL, D_EXPERT), 1),
        expert((D_EXPERT, D_MODEL), 1),
        expert((D_MODEL, D_EXPERT), 2),
        expert((D_MODEL, D_EXPERT), 2),
        expert((D_EXPERT, D_MODEL), 2),
        const2((1, D_MODEL)),
    ]
    grid_spec = pltpu.PrefetchScalarGridSpec(
        num_scalar_prefetch=3, grid=(max_tiles,), in_specs=in_specs,
        out_specs=pl.BlockSpec(memory_space=pl.ANY),
        scratch_shapes=[
            pltpu.VMEM((2 * TR * TOKEN_ROWS, LANES), F32),
            pltpu.VMEM((2 * TR * TOKEN_ROWS, LANES), F32),
            pltpu.VMEM((n // LANES, LANES), jnp.int32),
            pltpu.SMEM((n // LANES, LANES), jnp.int32),
            pltpu.SMEM((max_tiles * TR,), jnp.int32),
            pltpu.SemaphoreType.DMA((2,)),
            pltpu.SemaphoreType.DMA((2,)),
            pltpu.SemaphoreType.DMA(()),
        ])
    return pl.pallas_call(
        functools.partial(_moe_kernel, final=final, n_tokens=n),
        grid_spec=grid_spec,
        out_shape=jax.ShapeDtypeStruct((n * TOKEN_ROWS, LANES), F32),
        compiler_params=pltpu.CompilerParams(
            dimension_semantics=("arbitrary",), vmem_limit_bytes=VMEM_LIMIT),
        name=f"moe_l{layer}",
    )(tinfo, nt, tbase, packed, x2, p["norm2_g"], p["wr_cat"], p["rb"],
      p["exp_w_gate"], p["exp_w_up"], p["exp_w_down"], p["exp_w_gate"], p["exp_w_up"], p["exp_w_down"],
      p["final_norm_g"])


def _tile_tables(counts, n):
    max_tiles = n // TR + N_BUCKETS
    order = np.asarray([g * N_PAIRS + q for g in range(N_GROUPS) for q in PAIR_ORDER], np.int32)
    cnt = counts[:N_BUCKETS, 0][order]
    tiles = (cnt + TR - 1) // TR
    tile_end = jnp.cumsum(tiles)
    tile_start = tile_end - tiles
    nt = tile_end[-1:].astype(jnp.int32)
    tile_ids = jnp.arange(max_tiles, dtype=jnp.int32)
    tpos = jnp.sum((tile_ids[:, None] >= tile_end[None, :]).astype(jnp.int32), axis=1)
    tpos = jnp.minimum(tpos, N_BUCKETS - 1)
    grp = jnp.asarray(order // N_PAIRS)[tpos]
    pr = jnp.asarray(order % N_PAIRS)[tpos]
    ea = grp * EXPERTS_PER_GROUP + jnp.asarray(PAIR_A, jnp.int32)[pr]
    eb = grp * EXPERTS_PER_GROUP + jnp.asarray(PAIR_B, jnp.int32)[pr]
    valid = jnp.clip(cnt[tpos] - (tile_ids - tile_start[tpos]) * TR, 0, TR)
    tinfo = jnp.stack([grp, ea, eb, valid], axis=1).reshape(-1).astype(jnp.int32)
    tbase = (tile_start.astype(jnp.int32) * TR)[np.argsort(order)]
    return tinfo, nt, tbase


def kernel(x, norm1_g, w_in, gm_vnorm_g, gm_ws, gm_bs, gm_w_out, cf_dw_w, cf_dw_b, cf_ln_g, cf_ln_b,
           cf_w_out, sc_conv_w, sc_w_out, w_o, norm2_g, router_g, router_g_b, router_e, router_e_b,
           exp_w_gate, exp_w_up, exp_w_down, final_norm_g):
    b, s, d = x.shape
    depth = w_in.shape[0]
    n = b * s
    assert d == D_MODEL and s % TM == 0 and TM % GM_BLOCK == 0 and n % TR == 0
    assert n <= (1 << RANK_BITS) and TR % DMA_UNROLL == 0 and n % LANES == 0

    wr = jnp.concatenate([router_g, router_e], axis=-1)
    wr = jnp.pad(wr, ((0, 0), (0, 0), (0, LANES - wr.shape[-1])))
    wr_hi = wr.astype(BF16)
    wr_lo = (wr - wr_hi.astype(F32)).astype(BF16)
    rb = jnp.concatenate([router_g_b, router_e_b], axis=-1)
    rbt = jnp.broadcast_to(jnp.pad(rb, ((0, 0), (0, ROUTE_ROWS - rb.shape[-1])))[:, :, None],
                           (depth, ROUTE_ROWS, LANES))
    rb = jnp.pad(rb, ((0, 0), (0, LANES - rb.shape[-1])))[:, None, :]
    wrt = jnp.swapaxes(jnp.concatenate([wr_hi[:, :, :ROUTE_ROWS], wr_lo[:, :, :ROUTE_ROWS]], axis=-1), 1, 2)
    head = np.arange(GM_WIDTH) // GM_HEAD_DIM
    gmat = jnp.asarray((head[:, None] == head[None, :]).astype(np.float32) / GM_HEAD_DIM, BF16)

    p = dict(
        n_tokens=n,
        norm1_g=norm1_g[:, None, :],
        w_in=w_in,
        gm_vnorm_g=gm_vnorm_g[:, None, :],
        gmat=gmat,
        gm_ws=gm_ws,
        gm_bias=jnp.repeat(jnp.swapaxes(gm_bs, 1, 2), GM_HEAD_DIM, axis=2),
        gm_w_out=gm_w_out,
        cf_dw_w=jnp.repeat(cf_dw_w, SUBLANES, axis=1),
        cf_dw_b=cf_dw_b[:, None, :],
        cf_ln_g=cf_ln_g[:, None, :],
        cf_ln_b=cf_ln_b[:, None, :],
        cf_w_out=cf_w_out,
        sc_conv_w=sc_conv_w,
        sc_w_out=sc_w_out,
        w_o=w_o,
        norm2_g=norm2_g[:, None, :],
        wrt=wrt, rbt=rbt,
        wr_cat=jnp.concatenate([wr_hi, wr_lo], axis=-1), rb=rb,
        exp_w_gate=exp_w_gate,
        exp_w_up=exp_w_up,
        exp_w_down=exp_w_down,
        final_norm_g=final_norm_g[None, :],
    )

    xf = x.reshape(n, d)
    for layer in range(depth):
        x2, route, counts = _mixer_call(xf, layer, p, s // TM, rows_in=(layer > 0))
        tinfo, nt, tbase = _tile_tables(counts, n)
        xf = _moe_call(x2, layer, p, tinfo, nt, tbase, route.reshape(n // LANES, LANES),
                       final=(layer == depth - 1))
    return xf.reshape(b, s, d)
```

```python
import functools

import numpy as np
import jax
import jax.numpy as jnp
from jax import lax
from jax.experimental import pallas as pl
from jax.experimental.pallas import tpu as pltpu

F32 = jnp.float32
BF16 = jnp.bfloat16

D_MODEL = 1024
CHUNK = 64
GM_BLOCK = 128
GM_HEADS = 8
GM_WIDTH = 512
GM_HEAD_DIM = GM_WIDTH // GM_HEADS
CF_WIDTH = 512
CF_KERNEL = 31
SC_WIDTH = 512
SC_KERNEL = 3
N_BRANCH = 3
N_GROUPS = 4
EXPERTS_PER_GROUP = 4
N_EXPERTS = N_GROUPS * EXPERTS_PER_GROUP
D_EXPERT = 512
EPS = 1e-6

C_GU, C_GV, C_CA, C_CG, C_SB, C_SC, C_SH, C_GATES = 0, 512, 1024, 1536, 2048, 2560, 3072, 3584
W_IN_COLS = C_GATES + N_BRANCH * D_MODEL

LANES = 128
SUBLANES = 8
TOKEN_ROWS = D_MODEL // LANES
N_PAIRS = 6
N_BUCKETS = N_GROUPS * N_PAIRS
PAIR_A = (0, 0, 0, 1, 1, 3)
PAIR_B = (1, 2, 3, 2, 3, 2)
PAIR_ORDER = (0, 1, 2, 4, 3, 5)
RANK_BITS = 16
ROUTE_ROWS = 32

TM = 256
TR = 256
CF_TAIL = 32
SC_TAIL = 8
CONV_ROWS = 32
W_CHUNK = 512
SLAB_STRIDE = 2
DMA_UNROLL = 16
VMEM_LIMIT = 56 * 1024 * 1024


def _dot(a, b):
    return jnp.dot(a, b, preferred_element_type=F32)


def _dot_nt(a, b):
    return lax.dot_general(a, b, (((1,), (1,)), ((), ())), preferred_element_type=F32)


def _split_bf16(v):
    hi = v.astype(BF16)
    lo = (v - hi.astype(F32)).astype(BF16)
    return hi, lo


def _router_logits(x2, n2g, wr, rb):
    hb = (x2 * lax.rsqrt(jnp.mean(x2 * x2, axis=-1, keepdims=True) + EPS) * n2g).astype(BF16)
    return hb, _dot(hb, wr) + rb


def _load_token_rows(ref, row0, n):
    return jnp.concatenate(
        [ref[pl.ds(row0 + s, n, stride=TOKEN_ROWS), :] for s in range(TOKEN_ROWS)], axis=1)


def _store_token_rows(ref, row0, val):
    n = val.shape[0]
    for s in range(TOKEN_ROWS):
        ref[pl.ds(row0 + s, n, stride=TOKEN_ROWS), :] = val[:, s * LANES:(s + 1) * LANES]


def _mixer_kernel(x_ref, n1g_ref, w_in_hbm, vng_ref, gmat_ref, ws_ref, gbias_ref, gm_wo_f32,
                  dww_ref, dwb_ref, lng_ref, lnb_ref, cf_wo_f32, scw_ref, sc_wo_f32, w_o_f32,
                  n2g_ref, wrt_ref, rbt_ref,
                  x2_ref, route_ref, cnt_ref,
                  aslab, sslab, base, w_in_ref, gm_wo_ref, cf_wo_ref, sc_wo_ref, w_o_ref, stage, wsem,
                  *, layer, tiles_per_seq, rows_in):
    i = pl.program_id(0)
    first = (i % tiles_per_seq) == 0
    nb = TM // GM_BLOCK
    lane_groups = CF_WIDTH // LANES

    @pl.when(i == 0)
    def _():
        n_chunks = W_IN_COLS // W_CHUNK

        def chunk_copy(c):
            return pltpu.make_async_copy(
                w_in_hbm.at[layer, :, pl.ds(c * W_CHUNK, W_CHUNK)], stage.at[c % 2], wsem.at[c % 2])

        chunk_copy(0).start()
        for c in range(n_chunks):
            if c + 1 < n_chunks:
                chunk_copy(c + 1).start()
            chunk_copy(c).wait()
            w_in_ref[:, c * W_CHUNK:(c + 1) * W_CHUNK] = stage[c % 2].astype(BF16)
        gm_wo_ref[...] = gm_wo_f32[...].astype(BF16)
        cf_wo_ref[...] = cf_wo_f32[...].astype(BF16)
        sc_wo_ref[...] = sc_wo_f32[...].astype(BF16)
        w_o_ref[...] = w_o_f32[...].astype(BF16)

    class Slab:
        def __init__(self, ref, hist_rows):
            self.ref, self.hist_rows = ref, hist_rows

        def _idx(self, group, row, n):
            start = SLAB_STRIDE * (group * self.hist_rows + row)
            return (pl.ds(start, n, stride=SLAB_STRIDE), slice(None))

        def load(self, group, row, n):
            return self.ref[self._idx(group, row, n)]

        def store(self, group, row, val):
            self.ref[self._idx(group, row, val.shape[0])] = val

    a_hist = Slab(aslab, TM + CF_TAIL)
    s_hist = Slab(sslab, TM + SC_TAIL)

    @pl.when(first)
    def _():
        for g in range(lane_groups):
            a_hist.store(g, 0, jnp.zeros((CF_TAIL, LANES), F32))
            s_hist.store(g, 0, jnp.zeros((SC_TAIL, LANES), F32))

    @pl.when(jnp.logical_not(first))
    def _():
        for g in range(lane_groups):
            a_hist.store(g, 0, a_hist.load(g, TM, CF_TAIL))
            s_hist.store(g, 0, s_hist.load(g, TM, SC_TAIL))

    @pl.when(i == 0)
    def _():
        base[...] = jnp.zeros(base.shape, F32)

    x = _load_token_rows(x_ref, 0, TM) if rows_in else x_ref[...]
    h = x * lax.rsqrt(jnp.mean(x * x, axis=-1, keepdims=True) + EPS) * n1g_ref[...]
    hb = h.astype(BF16)

    def proj(c0, width):
        return _dot(hb, w_in_ref[:, c0:c0 + width])

    glu = proj(C_CA, CF_WIDTH) * jax.nn.sigmoid(proj(C_CG, CF_WIDTH))
    for g in range(lane_groups):
        a_hist.store(g, CF_TAIL, glu[:, g * LANES:(g + 1) * LANES])
    pu, pv = proj(C_GU, GM_WIDTH), proj(C_GV, GM_WIDTH)
    s_b, s_c, s_h = proj(C_SB, SC_WIDTH), proj(C_SC, SC_WIDTH), proj(C_SH, SC_WIDTH)

    ngrp = CONV_ROWS // SUBLANES
    conv_rows = []
    for c in range(TM // CONV_ROWS):
        acc = [[None] * lane_groups for _ in range(ngrp)]
        for k in range(CF_KERNEL):
            shift = k + CF_TAIL - (CF_KERNEL - 1)
            for g in range(lane_groups):
                wk = dww_ref[k * SUBLANES:(k + 1) * SUBLANES, g * LANES:(g + 1) * LANES]
                for r in range(ngrp):
                    val = a_hist.load(g, c * CONV_ROWS + r * SUBLANES + shift, SUBLANES)
                    acc[r][g] = wk * val if acc[r][g] is None else acc[r][g] + wk * val
        conv_rows.extend(jnp.concatenate(a, axis=1) for a in acc)

    sq_hi, sq_lo = _split_bf16(pv * pv)
    ms = _dot(sq_hi, gmat_ref[...]) + _dot(sq_lo, gmat_ref[...])
    cv = jnp.concatenate(conv_rows, axis=0) + dwb_ref[...]
    mu = jnp.mean(cv, axis=-1, keepdims=True)
    cc = cv - mu
    var = jnp.mean(cc * cc, axis=-1, keepdims=True)
    sc_in = s_c * s_h
    conv3 = []
    for g in range(lane_groups):
        s_hist.store(g, SC_TAIL, sc_in[:, g * LANES:(g + 1) * LANES])
        taps = [scw_ref[k:k + 1, g * LANES:(g + 1) * LANES] * s_hist.load(g, SC_TAIL - (SC_KERNEL - 1) + k, TM)
                for k in range(SC_KERNEL)]
        conv3.append((taps[0] + taps[1]) + taps[2])
    y_sc = _dot((s_b * jnp.concatenate(conv3, axis=1)).astype(BF16), sc_wo_ref[...])
    def route_tile(xp):
        h2 = xp * lax.rsqrt(jnp.mean(xp * xp, axis=-1, keepdims=True) + EPS) * n2g_ref[...]
        h_hi, h_lo = _split_bf16(h2)
        by_hi = _dot_nt(wrt_ref[...], h_hi)
        by_lo = _dot_nt(wrt_ref[0:ROUTE_ROWS, :], h_lo)
        lt = (by_hi[:ROUTE_ROWS] + by_hi[ROUTE_ROWS:]) + by_lo + rbt_ref[:, 0:1]
        row = [lt[j:j + 1, :] for j in range(N_GROUPS + N_EXPERTS)]

        def first_argmax(vals, skip=None):
            best_v = jnp.full_like(vals[0], -jnp.inf)
            best_i = jnp.zeros_like(vals[0])
            for j, vj in enumerate(vals):
                better = vj > best_v
                if skip is not None:
                    better = better & (skip != float(j))
                best_i = jnp.where(better, float(j), best_i)
                best_v = jnp.where(better, vj, best_v)
            return best_i, best_v

        gidx, _ = first_argmax(row[:N_GROUPS])
        in_group = [
            jnp.where(gidx == 0.0, row[N_GROUPS + j],
                      jnp.where(gidx == 1.0, row[N_GROUPS + EXPERTS_PER_GROUP + j],
                                jnp.where(gidx == 2.0, row[N_GROUPS + 2 * EXPERTS_PER_GROUP + j],
                                          row[N_GROUPS + 3 * EXPERTS_PER_GROUP + j])))
            for j in range(EXPERTS_PER_GROUP)]
        i1, _ = first_argmax(in_group)
        i2, _ = first_argmax(in_group, skip=i1)
        p_lo = jnp.minimum(i1, i2)
        p_hi = jnp.maximum(i1, i2)
        pair = p_lo * (7.0 - p_lo) * 0.5 + p_hi - p_lo - 1.0
        bucket = gidx * N_PAIRS + pair

        hit = lax.broadcasted_iota(jnp.int32, (ROUTE_ROWS, TM), 0).astype(F32) == bucket
        onehot = hit.astype(BF16)
        upper = (lax.broadcasted_iota(jnp.int32, (TM, TM), 0)
                 <= lax.broadcasted_iota(jnp.int32, (TM, TM), 1)).astype(BF16)
        prefix = _dot(onehot, upper)
        rank = jnp.sum(jnp.where(hit, prefix - 1.0 + base[:, 0:1], 0.0), axis=0, keepdims=True)
        base[...] = base[...] + jnp.sum(hit.astype(F32), axis=1, keepdims=True)
        packed = bucket * float(1 << RANK_BITS) + rank
        route_ref[...] = packed.astype(jnp.int32)
        cnt_ref[...] = base[...].astype(jnp.int32)

    g_gm, g_cf, g_sc = (proj(C_GATES + j * D_MODEL, D_MODEL) for j in range(N_BRANCH))
    ln = cc * lax.rsqrt(var + EPS) * lng_ref[...] + lnb_ref[...]
    y_cf = _dot(jax.nn.silu(ln).astype(BF16), cf_wo_ref[...])

    v = (pv * lax.rsqrt(ms + EPS) * vng_ref[...]).astype(BF16)
    ri = lax.broadcasted_iota(jnp.int32, (GM_BLOCK, GM_BLOCK), 0) // CHUNK
    ci = lax.broadcasted_iota(jnp.int32, (GM_BLOCK, GM_BLOCK), 1) // CHUNK
    causal = ci <= ri
    lane_in_pair = lax.broadcasted_iota(jnp.int32, (GM_BLOCK, nb * LANES), 1) % LANES
    first_head = lane_in_pair < GM_HEAD_DIM
    zparts = []
    for p in range(GM_HEADS // 2):
        w2 = jnp.concatenate(
            [jnp.where(causal, ws_ref[2 * p], 0.0), jnp.where(causal, ws_ref[2 * p + 1], 0.0)],
            axis=0).astype(BF16)
        vcat = jnp.concatenate(
            [v[n * GM_BLOCK:(n + 1) * GM_BLOCK, p * LANES:(p + 1) * LANES] for n in range(nb)], axis=1)
        r = _dot(w2, vcat)
        zparts.append(jnp.where(first_head, r[:GM_BLOCK], r[GM_BLOCK:]))
    z = jnp.concatenate(
        [jnp.concatenate([zp[:, n * LANES:(n + 1) * LANES] for zp in zparts], axis=1) + gbias_ref[...]
         for n in range(nb)], axis=0)
    y_gm = _dot((pu * z).astype(BF16), gm_wo_ref[...])

    m = (jax.nn.sigmoid(g_gm) * y_gm + jax.nn.sigmoid(g_cf) * y_cf) + jax.nn.sigmoid(g_sc) * y_sc
    x2 = x + _dot(m.astype(BF16), w_o_ref[...])
    _store_token_rows(x2_ref, 0, x2)
    route_tile(x2)


def _mixer_call(x, layer, p, tiles_per_seq, rows_in):
    n = p["n_tokens"]
    grid = (n // TM,)

    def const2(shape):
        return pl.BlockSpec(shape, lambda i: (0, 0))

    def layer2(shape):
        return pl.BlockSpec((None,) + shape, lambda i: (layer, 0, 0))

    def layer3(shape):
        return pl.BlockSpec((None,) + shape, lambda i: (layer, 0, 0, 0))

    x_spec = (pl.BlockSpec((TM * TOKEN_ROWS, LANES), lambda i: (i, 0)) if rows_in
              else pl.BlockSpec((TM, D_MODEL), lambda i: (i, 0)))
    in_specs = [
        x_spec,
        layer2((1, D_MODEL)),
        pl.BlockSpec(memory_space=pl.ANY),
        layer2((1, GM_WIDTH)),
        const2((GM_WIDTH, GM_WIDTH)),
        layer3((GM_HEADS, GM_BLOCK, GM_BLOCK)),
        layer2((GM_BLOCK, GM_WIDTH)),
        layer2((GM_WIDTH, D_MODEL)),
        layer2((CF_KERNEL * SUBLANES, CF_WIDTH)),
        layer2((1, CF_WIDTH)),
        layer2((1, CF_WIDTH)),
        layer2((1, CF_WIDTH)),
        layer2((CF_WIDTH, D_MODEL)),
        layer2((SC_KERNEL, SC_WIDTH)),
        layer2((SC_WIDTH, D_MODEL)),
        layer2((D_MODEL, D_MODEL)),
        layer2((1, D_MODEL)),
        layer2((2 * ROUTE_ROWS, D_MODEL)),
        layer2((ROUTE_ROWS, LANES)),
    ]
    out_specs = [
        pl.BlockSpec((TM * TOKEN_ROWS, LANES), lambda i: (i, 0)),
        pl.BlockSpec((None, 1, TM), lambda i: (i, 0, 0)),
        pl.BlockSpec((ROUTE_ROWS, LANES), lambda i: (0, 0)),
    ]
    out_shape = [
        jax.ShapeDtypeStruct((n * TOKEN_ROWS, LANES), F32),
        jax.ShapeDtypeStruct((n // TM, 1, TM), jnp.int32),
        jax.ShapeDtypeStruct((ROUTE_ROWS, LANES), jnp.int32),
    ]
    lane_groups = CF_WIDTH // LANES
    scratch = [
        pltpu.VMEM((lane_groups * SLAB_STRIDE * (TM + CF_TAIL), LANES), F32),
        pltpu.VMEM((lane_groups * SLAB_STRIDE * (TM + SC_TAIL), LANES), F32),
        pltpu.VMEM((ROUTE_ROWS, LANES), F32),
        pltpu.VMEM((D_MODEL, W_IN_COLS), BF16),
        pltpu.VMEM((GM_WIDTH, D_MODEL), BF16),
        pltpu.VMEM((CF_WIDTH, D_MODEL), BF16),
        pltpu.VMEM((SC_WIDTH, D_MODEL), BF16),
        pltpu.VMEM((D_MODEL, D_MODEL), BF16),
        pltpu.VMEM((2, D_MODEL, W_CHUNK), F32),
        pltpu.SemaphoreType.DMA((2,)),
    ]
    return pl.pallas_call(
        functools.partial(_mixer_kernel, layer=layer, tiles_per_seq=tiles_per_seq, rows_in=rows_in),
        grid=grid, in_specs=in_specs, out_specs=out_specs, out_shape=out_shape,
        scratch_shapes=scratch,
        compiler_params=pltpu.CompilerParams(
            dimension_semantics=("arbitrary",), vmem_limit_bytes=VMEM_LIMIT),
        name=f"mixer_l{layer}",
    )(x, p["norm1_g"], p["w_in"], p["gm_vnorm_g"], p["gmat"], p["gm_ws"], p["gm_bias"], p["gm_w_out"],
      p["cf_dw_w"], p["cf_dw_b"], p["cf_ln_g"], p["cf_ln_b"], p["cf_w_out"], p["sc_conv_w"],
      p["sc_w_out"], p["w_o"], p["norm2_g"], p["wrt"], p["rbt"])


def _moe_kernel(tinfo_ref, nt_ref, tbase_ref,
                packed_ref, x2_hbm, n2g_ref, wr_ref, rb_ref,
                wga_ref, wua_ref, wda_ref, wgb_ref, wub_ref, wdb_ref, fg_ref,
                out_hbm,
                xbuf, obuf, pos_v, pos_s, src_s, gsem, ssem, psem, *, final, n_tokens):
    t = pl.program_id(0)
    nt = nt_ref[0]
    slot = t % 2

    def valid_rows(tile):
        return tinfo_ref[4 * tile + 3]

    def token_rows(hbm, slot0, u):
        tok = src_s[slot0 + u]
        return hbm.at[pl.ds(pl.multiple_of(tok * TOKEN_ROWS, TOKEN_ROWS), TOKEN_ROWS), :]

    def buf_rows(buf, row0, u):
        return buf.at[pl.ds(pl.multiple_of((row0 + u) * TOKEN_ROWS, TOKEN_ROWS), TOKEN_ROWS), :]

    def gather_row(slot0, row0, u, sl):
        return pltpu.make_async_copy(token_rows(x2_hbm, slot0, u), buf_rows(xbuf, row0, u), gsem.at[sl])

    def scatter_row(slot0, row0, u, sl):
        return pltpu.make_async_copy(buf_rows(obuf, row0, u), token_rows(out_hbm, slot0, u), ssem.at[sl])

    def for_rows(make, tile, sl, act):
        count = valid_rows(tile)
        groups = lax.shift_right_logical(count, DMA_UNROLL.bit_length() - 1)
        slot0, row0 = tile * TR, sl * TR

        def group(g, c):
            for u in range(DMA_UNROLL):
                act(make(slot0 + g * DMA_UNROLL, row0 + g * DMA_UNROLL, u, sl))
            return c
        lax.fori_loop(0, groups, group, 0)

        def single(r, c):
            act(make(slot0, row0, r, sl))
            return c
        lax.fori_loop(groups * DMA_UNROLL, count, single, 0)

    def start_rows(make, tile, sl):
        for_rows(make, tile, sl, lambda cp: cp.start())

    def wait_rows(make, tile, sl):
        for_rows(make, tile, sl, lambda cp: cp.wait())

    @pl.when(t == 0)
    def _():
        word = packed_ref[...]
        bkt = lax.shift_right_logical(word, RANK_BITS)
        pos = word & ((1 << RANK_BITS) - 1)
        for b in range(N_BUCKETS):
            pos = pos + jnp.where(bkt == b, tbase_ref[b], 0)
        pos_v[...] = pos
        to_smem = pltpu.make_async_copy(pos_v, pos_s, psem)
        to_smem.start()
        to_smem.wait()

        def fill(r, c):
            for u in range(LANES):
                src_s[pos_s[r, u]] = r * LANES + u
            return c
        lax.fori_loop(0, n_tokens // LANES, fill, 0)
        xbuf[...] = jnp.zeros(xbuf.shape, F32)

    @pl.when(t < nt)
    def _():
        @pl.when(t == 0)
        def _():
            start_rows(gather_row, t, slot)

        @pl.when(t + 1 < nt)
        def _():
            start_rows(gather_row, t + 1, 1 - slot)

        wait_rows(gather_row, t, slot)

        @pl.when(t >= 2)
        def _():
            wait_rows(scatter_row, t - 2, slot)

        grp = tinfo_ref[4 * t]
        ea = tinfo_ref[4 * t + 1]
        eb = tinfo_ref[4 * t + 2]
        def experts(rows):
            xr = _load_token_rows(xbuf, slot * (TR * TOKEN_ROWS), rows)
            hb, logits = _router_logits(xr, n2g_ref[...], wr_ref[...], rb_ref[...])
            lane = lax.broadcasted_iota(jnp.int32, (rows, LANES), 1)
            is_g = lane < N_GROUPS
            gmax = jnp.max(jnp.where(is_g, logits, -jnp.inf), axis=-1, keepdims=True)
            eg = jnp.where(is_g, jnp.exp(logits - gmax), 0.0)
            p_top = (jnp.sum(jnp.where(lane == grp, eg, 0.0), axis=-1, keepdims=True)
                     / jnp.sum(eg, axis=-1, keepdims=True))
            la = jnp.sum(jnp.where(lane == N_GROUPS + ea, logits, 0.0), axis=-1, keepdims=True)
            lb = jnp.sum(jnp.where(lane == N_GROUPS + eb, logits, 0.0), axis=-1, keepdims=True)
            mx = jnp.maximum(la, lb)
            pa = jnp.exp(la - mx)
            pb = jnp.exp(lb - mx)
            wa = pa / (pa + pb) * p_top
            wb = pb / (pa + pb) * p_top

            def w(ref):
                return ref[...].astype(BF16)

            act_a = jax.nn.silu(_dot(hb, w(wga_ref))) * _dot(hb, w(wua_ref))
            act_b = jax.nn.silu(_dot(hb, w(wgb_ref))) * _dot(hb, w(wub_ref))
            y = _dot((wa * act_a).astype(BF16), w(wda_ref)) + _dot((wb * act_b).astype(BF16), w(wdb_ref))
            res = xr + y
            if final:
                res = res * lax.rsqrt(jnp.mean(res * res, axis=-1, keepdims=True) + EPS) * fg_ref[...]
            _store_token_rows(obuf, slot * (TR * TOKEN_ROWS), res)

        @pl.when(valid_rows(t) > TR // 2)
        def _():
            experts(TR)

        @pl.when(valid_rows(t) <= TR // 2)
        def _():
            experts(TR // 2)

        start_rows(scatter_row, t, slot)

        @pl.when(t == nt - 1)
        def _():
            @pl.when(t >= 1)
            def _():
                wait_rows(scatter_row, t - 1, 1 - slot)
            wait_rows(scatter_row, t, slot)


def _moe_call(x2, layer, p, tinfo, nt, tbase, packed, final):
    n = p["n_tokens"]
    max_tiles = n // TR + N_BUCKETS

    def const2(shape):
        return pl.BlockSpec(shape, lambda t, *_: (0, 0))

    def layer2(shape):
        return pl.BlockSpec((None,) + shape, lambda t, *_: (layer, 0, 0))

    def expert(shape, which):
        def imap(t, tinfo_ref, nt_ref, tbase_ref):
            tt = jnp.minimum(t, nt_ref[0] - 1)
            return (layer, tinfo_ref[4 * tt + which], 0, 0)
        return pl.BlockSpec((None, None) + shape, imap)

    in_specs = [
        const2((n // LANES, LANES)),
        pl.BlockSpec(memory_space=pl.ANY),
        layer2((1, D_MODEL)),
        layer2((D_MODEL, LANES)),
        layer2((1, LANES)),
        expert((D_MODEL, D_EXPERT), 1),
        expert((D_MODEL, D_EXPERT), 1),
        expert((D_EXPERT, D_MODEL), 1),
        expert((D_MODEL, D_EXPERT), 2),
        expert((D_MODEL, D_EXPERT), 2),
        expert((D_EXPERT, D_MODEL), 2),
        const2((1, D_MODEL)),
    ]
    grid_spec = pltpu.PrefetchScalarGridSpec(
        num_scalar_prefetch=3, grid=(max_tiles,), in_specs=in_specs,
        out_specs=pl.BlockSpec(memory_space=pl.ANY),
        scratch_shapes=[
            pltpu.VMEM((2 * TR * TOKEN_ROWS, LANES), F32),
            pltpu.VMEM((2 * TR * TOKEN_ROWS, LANES), F32),
            pltpu.VMEM((n // LANES, LANES), jnp.int32),
            pltpu.SMEM((n // LANES, LANES), jnp.int32),
            pltpu.SMEM((max_tiles * TR,), jnp.int32),
            pltpu.SemaphoreType.DMA((2,)),
            pltpu.SemaphoreType.DMA((2,)),
            pltpu.SemaphoreType.DMA(()),
        ])
    return pl.pallas_call(
        functools.partial(_moe_kernel, final=final, n_tokens=n),
        grid_spec=grid_spec,
        out_shape=jax.ShapeDtypeStruct((n * TOKEN_ROWS, LANES), F32),
        compiler_params=pltpu.CompilerParams(
            dimension_semantics=("arbitrary",), vmem_limit_bytes=VMEM_LIMIT),
        name=f"moe_l{layer}",
    )(tinfo, nt, tbase, packed, x2, p["norm2_g"], p["wr_hi"], p["rb"],
      p["exp_w_gate"], p["exp_w_up"], p["exp_w_down"], p["exp_w_gate"], p["exp_w_up"], p["exp_w_down"],
      p["final_norm_g"])


def _tile_tables(counts, n):
    max_tiles = n // TR + N_BUCKETS
    order = np.asarray([g * N_PAIRS + q for g in range(N_GROUPS) for q in PAIR_ORDER], np.int32)
    cnt = counts[:N_BUCKETS, 0][order]
    tiles = (cnt + TR - 1) // TR
    tile_end = jnp.cumsum(tiles)
    tile_start = tile_end - tiles
    nt = tile_end[-1:].astype(jnp.int32)
    tile_ids = jnp.arange(max_tiles, dtype=jnp.int32)
    tpos = jnp.sum((tile_ids[:, None] >= tile_end[None, :]).astype(jnp.int32), axis=1)
    tpos = jnp.minimum(tpos, N_BUCKETS - 1)
    grp = jnp.asarray(order // N_PAIRS)[tpos]
    pr = jnp.asarray(order % N_PAIRS)[tpos]
    ea = grp * EXPERTS_PER_GROUP + jnp.asarray(PAIR_A, jnp.int32)[pr]
    eb = grp * EXPERTS_PER_GROUP + jnp.asarray(PAIR_B, jnp.int32)[pr]
    valid = jnp.clip(cnt[tpos] - (tile_ids - tile_start[tpos]) * TR, 0, TR)
    tinfo = jnp.stack([grp, ea, eb, valid], axis=1).reshape(-1).astype(jnp.int32)
    tbase = (tile_start.astype(jnp.int32) * TR)[np.argsort(order)]
    return tinfo, nt, tbase


def kernel(x, norm1_g, w_in, gm_vnorm_g, gm_ws, gm_bs, gm_w_out, cf_dw_w, cf_dw_b, cf_ln_g, cf_ln_b,
           cf_w_out, sc_conv_w, sc_w_out, w_o, norm2_g, router_g, router_g_b, router_e, router_e_b,
           exp_w_gate, exp_w_up, exp_w_down, final_norm_g):
    b, s, d = x.shape
    depth = w_in.shape[0]
    n = b * s
    assert d == D_MODEL and s % TM == 0 and TM % GM_BLOCK == 0 and n % TR == 0
    assert n <= (1 << RANK_BITS) and TR % DMA_UNROLL == 0 and n % LANES == 0

    wr = jnp.concatenate([router_g, router_e], axis=-1)
    wr = jnp.pad(wr, ((0, 0), (0, 0), (0, LANES - wr.shape[-1])))
    wr_hi = wr.astype(BF16)
    wr_lo = (wr - wr_hi.astype(F32)).astype(BF16)
    rb = jnp.concatenate([router_g_b, router_e_b], axis=-1)
    rbt = jnp.broadcast_to(jnp.pad(rb, ((0, 0), (0, ROUTE_ROWS - rb.shape[-1])))[:, :, None],
                           (depth, ROUTE_ROWS, LANES))
    rb = jnp.pad(rb, ((0, 0), (0, LANES - rb.shape[-1])))[:, None, :]
    wrt = jnp.swapaxes(jnp.concatenate([wr_hi[:, :, :ROUTE_ROWS], wr_lo[:, :, :ROUTE_ROWS]], axis=-1), 1, 2)
    head = np.arange(GM_WIDTH) // GM_HEAD_DIM
    gmat = jnp.asarray((head[:, None] == head[None, :]).astype(np.float32) / GM_HEAD_DIM, BF16)

    p = dict(
        n_tokens=n,
        norm1_g=norm1_g[:, None, :],
        w_in=w_in,
        gm_vnorm_g=gm_vnorm_g[:, None, :],
        gmat=gmat,
        gm_ws=gm_ws,
        gm_bias=jnp.repeat(jnp.swapaxes(gm_bs, 1, 2), GM_HEAD_DIM, axis=2),
        gm_w_out=gm_w_out,
        cf_dw_w=jnp.repeat(cf_dw_w, SUBLANES, axis=1),
        cf_dw_b=cf_dw_b[:, None, :],
        cf_ln_g=cf_ln_g[:, None, :],
        cf_ln_b=cf_ln_b[:, None, :],
        cf_w_out=cf_w_out,
        sc_conv_w=sc_conv_w,
        sc_w_out=sc_w_out,
        w_o=w_o,
        norm2_g=norm2_g[:, None, :],
        wrt=wrt, rbt=rbt,
        wr_hi=wr_hi, rb=rb,
        exp_w_gate=exp_w_gate,
        exp_w_up=exp_w_up,
        exp_w_down=exp_w_down,
        final_norm_g=final_norm_g[None, :],
    )

    xf = x.reshape(n, d)
    for layer in range(depth):
        x2, route, counts = _mixer_call(xf, layer, p, s // TM, rows_in=(layer > 0))
        tinfo, nt, tbase = _tile_tables(counts, n)
        xf = _moe_call(x2, layer, p, tinfo, nt, tbase, route.reshape(n // LANES, LANES),
                       final=(layer == depth - 1))
    return xf.reshape(b, s, d)
```

```python
import functools

import numpy as np
import jax
import jax.numpy as jnp
from jax import lax
from jax.experimental import pallas as pl
from jax.experimental.pallas import tpu as pltpu

F32 = jnp.float32
BF16 = jnp.bfloat16

D_MODEL = 1024
CHUNK = 64
GM_BLOCK = 128
GM_HEADS = 8
GM_WIDTH = 512
GM_HEAD_DIM = GM_WIDTH // GM_HEADS
CF_WIDTH = 512
CF_KERNEL = 31
SC_WIDTH = 512
SC_KERNEL = 3
N_BRANCH = 3
N_GROUPS = 4
EXPERTS_PER_GROUP = 4
N_EXPERTS = N_GROUPS * EXPERTS_PER_GROUP
D_EXPERT = 512
EPS = 1e-6

C_GU, C_GV, C_CA, C_CG, C_SB, C_SC, C_SH, C_GATES = 0, 512, 1024, 1536, 2048, 2560, 3072, 3584
W_IN_COLS = C_GATES + N_BRANCH * D_MODEL

LANES = 128
SUBLANES = 8
TOKEN_ROWS = D_MODEL // LANES
N_PAIRS = 6
N_BUCKETS = N_GROUPS * N_PAIRS
PAIR_A = (0, 0, 0, 1, 1, 3)
PAIR_B = (1, 2, 3, 2, 3, 2)
PAIR_ORDER = (0, 1, 2, 4, 3, 5)
RANK_BITS = 16
ROUTE_ROWS = 32

TM = 256
TR = 256
CF_TAIL = 32
SC_TAIL = 8
CONV_ROWS = 32
W_CHUNK = 512
SLAB_STRIDE = 2
DMA_UNROLL = 16
VMEM_LIMIT = 56 * 1024 * 1024


def _dot(a, b):
    return jnp.dot(a, b, preferred_element_type=F32)


def _dot_nt(a, b):
    return lax.dot_general(a, b, (((1,), (1,)), ((), ())), preferred_element_type=F32)


def _split_bf16(v):
    hi = v.astype(BF16)
    lo = (v - hi.astype(F32)).astype(BF16)
    return hi, lo


def _router_logits(x2, n2g, wr, rb):
    hb = (x2 * lax.rsqrt(jnp.mean(x2 * x2, axis=-1, keepdims=True) + EPS) * n2g).astype(BF16)
    return hb, _dot(hb, wr) + rb


def _load_token_rows(ref, row0, n):
    return jnp.concatenate(
        [ref[pl.ds(row0 + s, n, stride=TOKEN_ROWS), :] for s in range(TOKEN_ROWS)], axis=1)


def _store_token_rows(ref, row0, val):
    n = val.shape[0]
    for s in range(TOKEN_ROWS):
        ref[pl.ds(row0 + s, n, stride=TOKEN_ROWS), :] = val[:, s * LANES:(s + 1) * LANES]


def _mixer_kernel(x_ref, n1g_ref, w_in_hbm, vng_ref, gmat_ref, ws_ref, gbias_ref, gm_wo_f32,
                  dww_ref, dwb_ref, lng_ref, lnb_ref, cf_wo_f32, scw_ref, sc_wo_f32, w_o_f32,
                  n2g_ref, wrt_ref, rbt_ref,
                  x2_ref, route_ref, cnt_ref,
                  aslab, sslab, base, w_in_ref, gm_wo_ref, cf_wo_ref, sc_wo_ref, w_o_ref, stage, wsem,
                  *, layer, tiles_per_seq, rows_in):
    i = pl.program_id(0)
    first = (i % tiles_per_seq) == 0
    nb = TM // GM_BLOCK
    lane_groups = CF_WIDTH // LANES

    @pl.when(i == 0)
    def _():
        n_chunks = W_IN_COLS // W_CHUNK

        def chunk_copy(c):
            return pltpu.make_async_copy(
                w_in_hbm.at[layer, :, pl.ds(c * W_CHUNK, W_CHUNK)], stage.at[c % 2], wsem.at[c % 2])

        chunk_copy(0).start()
        for c in range(n_chunks):
            if c + 1 < n_chunks:
                chunk_copy(c + 1).start()
            chunk_copy(c).wait()
            w_in_ref[:, c * W_CHUNK:(c + 1) * W_CHUNK] = stage[c % 2].astype(BF16)
        gm_wo_ref[...] = gm_wo_f32[...].astype(BF16)
        cf_wo_ref[...] = cf_wo_f32[...].astype(BF16)
        sc_wo_ref[...] = sc_wo_f32[...].astype(BF16)
        w_o_ref[...] = w_o_f32[...].astype(BF16)

    class Slab:
        def __init__(self, ref, hist_rows):
            self.ref, self.hist_rows = ref, hist_rows

        def _idx(self, group, row, n):
            start = SLAB_STRIDE * (group * self.hist_rows + row)
            return (pl.ds(start, n, stride=SLAB_STRIDE), slice(None))

        def load(self, group, row, n):
            return self.ref[self._idx(group, row, n)]

        def store(self, group, row, val):
            self.ref[self._idx(group, row, val.shape[0])] = val

    a_hist = Slab(aslab, TM + CF_TAIL)
    s_hist = Slab(sslab, TM + SC_TAIL)

    @pl.when(first)
    def _():
        for g in range(lane_groups):
            a_hist.store(g, 0, jnp.zeros((CF_TAIL, LANES), F32))
            s_hist.store(g, 0, jnp.zeros((SC_TAIL, LANES), F32))

    @pl.when(jnp.logical_not(first))
    def _():
        for g in range(lane_groups):
            a_hist.store(g, 0, a_hist.load(g, TM, CF_TAIL))
            s_hist.store(g, 0, s_hist.load(g, TM, SC_TAIL))

    @pl.when(i == 0)
    def _():
        base[...] = jnp.zeros(base.shape, F32)

    x = _load_token_rows(x_ref, 0, TM) if rows_in else x_ref[...]
    h = x * lax.rsqrt(jnp.mean(x * x, axis=-1, keepdims=True) + EPS) * n1g_ref[...]
    hb = h.astype(BF16)

    def proj(c0, width):
        return _dot(hb, w_in_ref[:, c0:c0 + width])

    glu = proj(C_CA, CF_WIDTH) * jax.nn.sigmoid(proj(C_CG, CF_WIDTH))
    for g in range(lane_groups):
        a_hist.store(g, CF_TAIL, glu[:, g * LANES:(g + 1) * LANES])
    pu, pv = proj(C_GU, GM_WIDTH), proj(C_GV, GM_WIDTH)
    s_b, s_c, s_h = proj(C_SB, SC_WIDTH), proj(C_SC, SC_WIDTH), proj(C_SH, SC_WIDTH)

    ngrp = CONV_ROWS // SUBLANES
    conv_rows = []
    for c in range(TM // CONV_ROWS):
        acc = [[None] * lane_groups for _ in range(ngrp)]
        for k in range(CF_KERNEL):
            shift = k + CF_TAIL - (CF_KERNEL - 1)
            for g in range(lane_groups):
                wk = dww_ref[k * SUBLANES:(k + 1) * SUBLANES, g * LANES:(g + 1) * LANES]
                for r in range(ngrp):
                    val = a_hist.load(g, c * CONV_ROWS + r * SUBLANES + shift, SUBLANES)
                    acc[r][g] = wk * val if acc[r][g] is None else acc[r][g] + wk * val
        conv_rows.extend(jnp.concatenate(a, axis=1) for a in acc)

    sq_hi, sq_lo = _split_bf16(pv * pv)
    ms = _dot(sq_hi, gmat_ref[...]) + _dot(sq_lo, gmat_ref[...])
    cv = jnp.concatenate(conv_rows, axis=0) + dwb_ref[...]
    mu = jnp.mean(cv, axis=-1, keepdims=True)
    cc = cv - mu
    var = jnp.mean(cc * cc, axis=-1, keepdims=True)
    sc_in = s_c * s_h
    conv3 = []
    for g in range(lane_groups):
        s_hist.store(g, SC_TAIL, sc_in[:, g * LANES:(g + 1) * LANES])
        taps = [scw_ref[k:k + 1, g * LANES:(g + 1) * LANES] * s_hist.load(g, SC_TAIL - (SC_KERNEL - 1) + k, TM)
                for k in range(SC_KERNEL)]
        conv3.append((taps[0] + taps[1]) + taps[2])
    y_sc = _dot((s_b * jnp.concatenate(conv3, axis=1)).astype(BF16), sc_wo_ref[...])
    def route_tile(xp):
        h2 = xp * lax.rsqrt(jnp.mean(xp * xp, axis=-1, keepdims=True) + EPS) * n2g_ref[...]
        h_hi, h_lo = _split_bf16(h2)
        by_hi = _dot_nt(wrt_ref[...], h_hi)
        by_lo = _dot_nt(wrt_ref[0:ROUTE_ROWS, :], h_lo)
        lt = (by_hi[:ROUTE_ROWS] + by_hi[ROUTE_ROWS:]) + by_lo + rbt_ref[:, 0:1]
        row = [lt[j:j + 1, :] for j in range(N_GROUPS + N_EXPERTS)]

        def first_argmax(vals, skip=None):
            best_v = jnp.full_like(vals[0], -jnp.inf)
            best_i = jnp.zeros_like(vals[0])
            for j, vj in enumerate(vals):
                better = vj > best_v
                if skip is not None:
                    better = better & (skip != float(j))
                best_i = jnp.where(better, float(j), best_i)
                best_v = jnp.where(better, vj, best_v)
            return best_i, best_v

        gidx, _ = first_argmax(row[:N_GROUPS])
        in_group = [
            jnp.where(gidx == 0.0, row[N_GROUPS + j],
                      jnp.where(gidx == 1.0, row[N_GROUPS + EXPERTS_PER_GROUP + j],
                                jnp.where(gidx == 2.0, row[N_GROUPS + 2 * EXPERTS_PER_GROUP + j],
                                          row[N_GROUPS + 3 * EXPERTS_PER_GROUP + j])))
            for j in range(EXPERTS_PER_GROUP)]
        i1, _ = first_argmax(in_group)
        i2, _ = first_argmax(in_group, skip=i1)
        p_lo = jnp.minimum(i1, i2)
        p_hi = jnp.maximum(i1, i2)
        pair = p_lo * (7.0 - p_lo) * 0.5 + p_hi - p_lo - 1.0
        bucket = gidx * N_PAIRS + pair

        hit = lax.broadcasted_iota(jnp.int32, (ROUTE_ROWS, TM), 0).astype(F32) == bucket
        onehot = hit.astype(BF16)
        upper = (lax.broadcasted_iota(jnp.int32, (TM, TM), 0)
                 <= lax.broadcasted_iota(jnp.int32, (TM, TM), 1)).astype(BF16)
        prefix = _dot(onehot, upper)
        rank = jnp.sum(jnp.where(hit, prefix - 1.0 + base[:, 0:1], 0.0), axis=0, keepdims=True)
        base[...] = base[...] + jnp.sum(hit.astype(F32), axis=1, keepdims=True)
        packed = bucket * float(1 << RANK_BITS) + rank
        route_ref[...] = packed.astype(jnp.int32)
        cnt_ref[...] = base[...].astype(jnp.int32)

    g_gm, g_cf, g_sc = (proj(C_GATES + j * D_MODEL, D_MODEL) for j in range(N_BRANCH))
    ln = cc * lax.rsqrt(var + EPS) * lng_ref[...] + lnb_ref[...]
    y_cf = _dot(jax.nn.silu(ln).astype(BF16), cf_wo_ref[...])

    v = (pv * lax.rsqrt(ms + EPS) * vng_ref[...]).astype(BF16)
    ri = lax.broadcasted_iota(jnp.int32, (GM_BLOCK, GM_BLOCK), 0) // CHUNK
    ci = lax.broadcasted_iota(jnp.int32, (GM_BLOCK, GM_BLOCK), 1) // CHUNK
    causal = ci <= ri
    lane_in_pair = lax.broadcasted_iota(jnp.int32, (GM_BLOCK, nb * LANES), 1) % LANES
    first_head = lane_in_pair < GM_HEAD_DIM
    zparts = []
    for p in range(GM_HEADS // 2):
        w2 = jnp.concatenate(
            [jnp.where(causal, ws_ref[2 * p], 0.0), jnp.where(causal, ws_ref[2 * p + 1], 0.0)],
            axis=0).astype(BF16)
        vcat = jnp.concatenate(
            [v[n * GM_BLOCK:(n + 1) * GM_BLOCK, p * LANES:(p + 1) * LANES] for n in range(nb)], axis=1)
        r = _dot(w2, vcat)
        zparts.append(jnp.where(first_head, r[:GM_BLOCK], r[GM_BLOCK:]))
    z = jnp.concatenate(
        [jnp.concatenate([zp[:, n * LANES:(n + 1) * LANES] for zp in zparts], axis=1) + gbias_ref[...]
         for n in range(nb)], axis=0)
    y_gm = _dot((pu * z).astype(BF16), gm_wo_ref[...])

    m = (jax.nn.sigmoid(g_gm) * y_gm + jax.nn.sigmoid(g_cf) * y_cf) + jax.nn.sigmoid(g_sc) * y_sc
    x2 = x + _dot(m.astype(BF16), w_o_ref[...])
    _store_token_rows(x2_ref, 0, x2)
    route_tile(x2)


def _mixer_call(x, layer, p, tiles_per_seq, rows_in):
    n = p["n_tokens"]
    grid = (n // TM,)

    def const2(shape):
        return pl.BlockSpec(shape, lambda i: (0, 0))

    def layer2(shape):
        return pl.BlockSpec((None,) + shape, lambda i: (layer, 0, 0))

    def layer3(shape):
        return pl.BlockSpec((None,) + shape, lambda i: (layer, 0, 0, 0))

    x_spec = (pl.BlockSpec((TM * TOKEN_ROWS, LANES), lambda i: (i, 0)) if rows_in
              else pl.BlockSpec((TM, D_MODEL), lambda i: (i, 0)))
    in_specs = [
        x_spec,
        layer2((1, D_MODEL)),
        pl.BlockSpec(memory_space=pl.ANY),
        layer2((1, GM_WIDTH)),
        const2((GM_WIDTH, GM_WIDTH)),
        layer3((GM_HEADS, GM_BLOCK, GM_BLOCK)),
        layer2((GM_BLOCK, GM_WIDTH)),
        layer2((GM_WIDTH, D_MODEL)),
        layer2((CF_KERNEL * SUBLANES, CF_WIDTH)),
        layer2((1, CF_WIDTH)),
        layer2((1, CF_WIDTH)),
        layer2((1, CF_WIDTH)),
        layer2((CF_WIDTH, D_MODEL)),
        layer2((SC_KERNEL, SC_WIDTH)),
        layer2((SC_WIDTH, D_MODEL)),
        layer2((D_MODEL, D_MODEL)),
        layer2((1, D_MODEL)),
        layer2((2 * ROUTE_ROWS, D_MODEL)),
        layer2((ROUTE_ROWS, LANES)),
    ]
    out_specs = [
        pl.BlockSpec((TM * TOKEN_ROWS, LANES), lambda i: (i, 0)),
        pl.BlockSpec((None, 1, TM), lambda i: (i, 0, 0)),
        pl.BlockSpec((ROUTE_ROWS, LANES), lambda i: (0, 0)),
    ]
    out_shape = [
        jax.ShapeDtypeStruct((n * TOKEN_ROWS, LANES), F32),
        jax.ShapeDtypeStruct((n // TM, 1, TM), jnp.int32),
        jax.ShapeDtypeStruct((ROUTE_ROWS, LANES), jnp.int32),
    ]
    lane_groups = CF_WIDTH // LANES
    scratch = [
        pltpu.VMEM((lane_groups * SLAB_STRIDE * (TM + CF_TAIL), LANES), F32),
        pltpu.VMEM((lane_groups * SLAB_STRIDE * (TM + SC_TAIL), LANES), F32),
        pltpu.VMEM((ROUTE_ROWS, LANES), F32),
        pltpu.VMEM((D_MODEL, W_IN_COLS), BF16),
        pltpu.VMEM((GM_WIDTH, D_MODEL), BF16),
        pltpu.VMEM((CF_WIDTH, D_MODEL), BF16),
        pltpu.VMEM((SC_WIDTH, D_MODEL), BF16),
        pltpu.VMEM((D_MODEL, D_MODEL), BF16),
        pltpu.VMEM((2, D_MODEL, W_CHUNK), F32),
        pltpu.SemaphoreType.DMA((2,)),
    ]
    return pl.pallas_call(
        functools.partial(_mixer_kernel, layer=layer, tiles_per_seq=tiles_per_seq, rows_in=rows_in),
        grid=grid, in_specs=in_specs, out_specs=out_specs, out_shape=out_shape,
        scratch_shapes=scratch,
        compiler_params=pltpu.CompilerParams(
            dimension_semantics=("arbitrary",), vmem_limit_bytes=VMEM_LIMIT),
        name=f"mixer_l{layer}",
    )(x, p["norm1_g"], p["w_in"], p["gm_vnorm_g"], p["gmat"], p["gm_ws"], p["gm_bias"], p["gm_w_out"],
      p["cf_dw_w"], p["cf_dw_b"], p["cf_ln_g"], p["cf_ln_b"], p["cf_w_out"], p["sc_conv_w"],
      p["sc_w_out"], p["w_o"], p["norm2_g"], p["wrt"], p["rbt"])


def _moe_kernel(tinfo_ref, nt_ref, tbase_ref,
                packed_ref, x2_hbm, n2g_ref, wr_ref, rb_ref,
                wga_ref, wua_ref, wda_ref, wgb_ref, wub_ref, wdb_ref, fg_ref,
                out_hbm,
                xbuf, obuf, pos_v, pos_s, src_s, gsem, ssem, psem, *, final, n_tokens):
    t = pl.program_id(0)
    nt = nt_ref[0]
    slot = t % 2

    def valid_rows(tile):
        return tinfo_ref[4 * tile + 3]

    def token_rows(hbm, slot0, u):
        tok = src_s[slot0 + u]
        return hbm.at[pl.ds(pl.multiple_of(tok * TOKEN_ROWS, TOKEN_ROWS), TOKEN_ROWS), :]

    def buf_rows(buf, row0, u):
        return buf.at[pl.ds(pl.multiple_of((row0 + u) * TOKEN_ROWS, TOKEN_ROWS), TOKEN_ROWS), :]

    def gather_row(slot0, row0, u, sl):
        return pltpu.make_async_copy(token_rows(x2_hbm, slot0, u), buf_rows(xbuf, row0, u), gsem.at[sl])

    def scatter_row(slot0, row0, u, sl):
        return pltpu.make_async_copy(buf_rows(obuf, row0, u), token_rows(out_hbm, slot0, u), ssem.at[sl])

    def for_rows(make, tile, sl, act):
        count = valid_rows(tile)
        groups = lax.shift_right_logical(count, DMA_UNROLL.bit_length() - 1)
        slot0, row0 = tile * TR, sl * TR

        def group(g, c):
            for u in range(DMA_UNROLL):
                act(make(slot0 + g * DMA_UNROLL, row0 + g * DMA_UNROLL, u, sl), u)
            return c
        lax.fori_loop(0, groups, group, 0)

        def single(r, c):
            act(make(slot0, row0, r, sl), 0)
            return c
        lax.fori_loop(groups * DMA_UNROLL, count, single, 0)

    def start_rows(make, tile, sl):
        for_rows(make, tile, sl, lambda cp, u: cp.start(priority=u % 2))

    def wait_rows(make, tile, sl):
        for_rows(make, tile, sl, lambda cp, u: cp.wait())

    @pl.when(t == 0)
    def _():
        word = packed_ref[...]
        bkt = lax.shift_right_logical(word, RANK_BITS)
        pos = word & ((1 << RANK_BITS) - 1)
        for b in range(N_BUCKETS):
            pos = pos + jnp.where(bkt == b, tbase_ref[b], 0)
        pos_v[...] = pos
        to_smem = pltpu.make_async_copy(pos_v, pos_s, psem)
        to_smem.start()
        to_smem.wait()

        def fill(r, c):
            for u in range(LANES):
                src_s[pos_s[r, u]] = r * LANES + u
            return c
        lax.fori_loop(0, n_tokens // LANES, fill, 0)
        xbuf[...] = jnp.zeros(xbuf.shape, F32)

    @pl.when(t < nt)
    def _():
        @pl.when(t == 0)
        def _():
            start_rows(gather_row, t, slot)

        @pl.when(t + 1 < nt)
        def _():
            start_rows(gather_row, t + 1, 1 - slot)

        wait_rows(gather_row, t, slot)

        @pl.when(t >= 2)
        def _():
            wait_rows(scatter_row, t - 2, slot)

        grp = tinfo_ref[4 * t]
        ea = tinfo_ref[4 * t + 1]
        eb = tinfo_ref[4 * t + 2]
        def experts(rows):
            xr = _load_token_rows(xbuf, slot * (TR * TOKEN_ROWS), rows)
            hb, logits = _router_logits(xr, n2g_ref[...], wr_ref[...], rb_ref[...])
            lane = lax.broadcasted_iota(jnp.int32, (rows, LANES), 1)
            is_g = lane < N_GROUPS
            gmax = jnp.max(jnp.where(is_g, logits, -jnp.inf), axis=-1, keepdims=True)
            eg = jnp.where(is_g, jnp.exp(logits - gmax), 0.0)
            p_top = (jnp.sum(jnp.where(lane == grp, eg, 0.0), axis=-1, keepdims=True)
                     / jnp.sum(eg, axis=-1, keepdims=True))
            la = jnp.sum(jnp.where(lane == N_GROUPS + ea, logits, 0.0), axis=-1, keepdims=True)
            lb = jnp.sum(jnp.where(lane == N_GROUPS + eb, logits, 0.0), axis=-1, keepdims=True)
            mx = jnp.maximum(la, lb)
            pa = jnp.exp(la - mx)
            pb = jnp.exp(lb - mx)
            wa = pa / (pa + pb) * p_top
            wb = pb / (pa + pb) * p_top

            def w(ref):
                return ref[...].astype(BF16)

            act_a = jax.nn.silu(_dot(hb, w(wga_ref))) * _dot(hb, w(wua_ref))
            act_b = jax.nn.silu(_dot(hb, w(wgb_ref))) * _dot(hb, w(wub_ref))
            y = _dot((wa * act_a).astype(BF16), w(wda_ref)) + _dot((wb * act_b).astype(BF16), w(wdb_ref))
            res = xr + y
            if final:
                res = res * lax.rsqrt(jnp.mean(res * res, axis=-1, keepdims=True) + EPS) * fg_ref[...]
            _store_token_rows(obuf, slot * (TR * TOKEN_ROWS), res)

        @pl.when(valid_rows(t) > TR // 2)
        def _():
            experts(TR)

        @pl.when(valid_rows(t) <= TR // 2)
        def _():
            experts(TR // 2)

        start_rows(scatter_row, t, slot)

        @pl.when(t == nt - 1)
        def _():
            @pl.when(t >= 1)
            def _():
                wait_rows(scatter_row, t - 1, 1 - slot)
            wait_rows(scatter_row, t, slot)


def _moe_call(x2, layer, p, tinfo, nt, tbase, packed, final):
    n = p["n_tokens"]
    max_tiles = n // TR + N_BUCKETS

    def const2(shape):
        return pl.BlockSpec(shape, lambda t, *_: (0, 0))

    def layer2(shape):
        return pl.BlockSpec((None,) + shape, lambda t, *_: (layer, 0, 0))

    def expert(shape, which):
        def imap(t, tinfo_ref, nt_ref, tbase_ref):
            tt = jnp.minimum(t, nt_ref[0] - 1)
            return (layer, tinfo_ref[4 * tt + which], 0, 0)
        return pl.BlockSpec((None, None) + shape, imap)

    in_specs = [
        const2((n // LANES, LANES)),
        pl.BlockSpec(memory_space=pl.ANY),
        layer2((1, D_MODEL)),
        layer2((D_MODEL, LANES)),
        layer2((1, LANES)),
        expert((D_MODEL, D_EXPERT), 1),
        expert((D_MODEL, D_EXPERT), 1),
        expert((D_EXPERT, D_MODEL), 1),
        expert((D_MODEL, D_EXPERT), 2),
        expert((D_MODEL, D_EXPERT), 2),
        expert((D_EXPERT, D_MODEL), 2),
        const2((1, D_MODEL)),
    ]
    grid_spec = pltpu.PrefetchScalarGridSpec(
        num_scalar_prefetch=3, grid=(max_tiles,), in_specs=in_specs,
        out_specs=pl.BlockSpec(memory_space=pl.ANY),
        scratch_shapes=[
            pltpu.VMEM((2 * TR * TOKEN_ROWS, LANES), F32),
            pltpu.VMEM((2 * TR * TOKEN_ROWS, LANES), F32),
            pltpu.VMEM((n // LANES, LANES), jnp.int32),
            pltpu.SMEM((n // LANES, LANES), jnp.int32),
            pltpu.SMEM((max_tiles * TR,), jnp.int32),
            pltpu.SemaphoreType.DMA((2,)),
            pltpu.SemaphoreType.DMA((2,)),
            pltpu.SemaphoreType.DMA(()),
        ])
    return pl.pallas_call(
        functools.partial(_moe_kernel, final=final, n_tokens=n),
        grid_spec=grid_spec,
        out_shape=jax.ShapeDtypeStruct((n * TOKEN_ROWS, LANES), F32),
        compiler_params=pltpu.CompilerParams(
            dimension_semantics=("arbitrary",), vmem_limit_bytes=VMEM_LIMIT),
        name=f"moe_l{layer}",
    )(tinfo, nt, tbase, packed, x2, p["norm2_g"], p["wr_hi"], p["rb"],
      p["exp_w_gate"], p["exp_w_up"], p["exp_w_down"], p["exp_w_gate"], p["exp_w_up"], p["exp_w_down"],
      p["final_norm_g"])


def _tile_tables(counts, n):
    max_tiles = n // TR + N_BUCKETS
    order = np.asarray([g * N_PAIRS + q for g in range(N_GROUPS) for q in PAIR_ORDER], np.int32)
    cnt = counts[:N_BUCKETS, 0][order]
    tiles = (cnt + TR - 1) // TR
    tile_end = jnp.cumsum(tiles)
    tile_start = tile_end - tiles
    nt = tile_end[-1:].astype(jnp.int32)
    tile_ids = jnp.arange(max_tiles, dtype=jnp.int32)
    tpos = jnp.sum((tile_ids[:, None] >= tile_end[None, :]).astype(jnp.int32), axis=1)
    tpos = jnp.minimum(tpos, N_BUCKETS - 1)
    grp = jnp.asarray(order // N_PAIRS)[tpos]
    pr = jnp.asarray(order % N_PAIRS)[tpos]
    ea = grp * EXPERTS_PER_GROUP + jnp.asarray(PAIR_A, jnp.int32)[pr]
    eb = grp * EXPERTS_PER_GROUP + jnp.asarray(PAIR_B, jnp.int32)[pr]
    valid = jnp.clip(cnt[tpos] - (tile_ids - tile_start[tpos]) * TR, 0, TR)
    tinfo = jnp.stack([grp, ea, eb, valid], axis=1).reshape(-1).astype(jnp.int32)
    tbase = (tile_start.astype(jnp.int32) * TR)[np.argsort(order)]
    return tinfo, nt, tbase


def kernel(x, norm1_g, w_in, gm_vnorm_g, gm_ws, gm_bs, gm_w_out, cf_dw_w, cf_dw_b, cf_ln_g, cf_ln_b,
           cf_w_out, sc_conv_w, sc_w_out, w_o, norm2_g, router_g, router_g_b, router_e, router_e_b,
           exp_w_gate, exp_w_up, exp_w_down, final_norm_g):
    b, s, d = x.shape
    depth = w_in.shape[0]
    n = b * s
    assert d == D_MODEL and s % TM == 0 and TM % GM_BLOCK == 0 and n % TR == 0
    assert n <= (1 << RANK_BITS) and TR % DMA_UNROLL == 0 and n % LANES == 0

    wr = jnp.concatenate([router_g, router_e], axis=-1)
    wr = jnp.pad(wr, ((0, 0), (0, 0), (0, LANES - wr.shape[-1])))
    wr_hi = wr.astype(BF16)
    wr_lo = (wr - wr_hi.astype(F32)).astype(BF16)
    rb = jnp.concatenate([router_g_b, router_e_b], axis=-1)
    rbt = jnp.broadcast_to(jnp.pad(rb, ((0, 0), (0, ROUTE_ROWS - rb.shape[-1])))[:, :, None],
                           (depth, ROUTE_ROWS, LANES))
    rb = jnp.pad(rb, ((0, 0), (0, LANES - rb.shape[-1])))[:, None, :]
    wrt = jnp.swapaxes(jnp.concatenate([wr_hi[:, :, :ROUTE_ROWS], wr_lo[:, :, :ROUTE_ROWS]], axis=-1), 1, 2)
    head = np.arange(GM_WIDTH) // GM_HEAD_DIM
    gmat = jnp.asarray((head[:, None] == head[None, :]).astype(np.float32) / GM_HEAD_DIM, BF16)

    p = dict(
        n_tokens=n,
        norm1_g=norm1_g[:, None, :],
        w_in=w_in,
        gm_vnorm_g=gm_vnorm_g[:, None, :],
        gmat=gmat,
        gm_ws=gm_ws,
        gm_bias=jnp.repeat(jnp.swapaxes(gm_bs, 1, 2), GM_HEAD_DIM, axis=2),
        gm_w_out=gm_w_out,
        cf_dw_w=jnp.repeat(cf_dw_w, SUBLANES, axis=1),
        cf_dw_b=cf_dw_b[:, None, :],
        cf_ln_g=cf_ln_g[:, None, :],
        cf_ln_b=cf_ln_b[:, None, :],
        cf_w_out=cf_w_out,
        sc_conv_w=sc_conv_w,
        sc_w_out=sc_w_out,
        w_o=w_o,
        norm2_g=norm2_g[:, None, :],
        wrt=wrt, rbt=rbt,
        wr_hi=wr_hi, rb=rb,
        exp_w_gate=exp_w_gate,
        exp_w_up=exp_w_up,
        exp_w_down=exp_w_down,
        final_norm_g=final_norm_g[None, :],
    )

    xf = x.reshape(n, d)
    for layer in range(depth):
        x2, route, counts = _mixer_call(xf, layer, p, s // TM, rows_in=(layer > 0))
        tinfo, nt, tbase = _tile_tables(counts, n)
        xf = _moe_call(x2, layer, p, tinfo, nt, tbase, route.reshape(n // LANES, LANES),
                       final=(layer == depth - 1))
    return xf.reshape(b, s, d)
```
